```python
import jax, jax.numpy as jnp
from jax import lax
import numpy as np

D_MODEL = 4096
BATCH = 1
SEQ = 16384
DEPTH = 4

HEAD_DIM = 128
NSA_HEADS = D_MODEL // (2 * HEAD_DIM)
NSA_KV_GROUPS = 4
NSA_HPG = NSA_HEADS // NSA_KV_GROUPS
NSA_WIDTH = NSA_HEADS * HEAD_DIM
NSA_KV_WIDTH = NSA_KV_GROUPS * HEAD_DIM
CMP_BLOCK = 32
CMP_STRIDE = 16
CMP_HIDDEN = 256
SEL_BLOCK = 64
SEL_TOPK = 16
SEL_LOCAL = 2
SEL_FORCE_SCORE = 1e4
WINDOW = 512
Q_BLOCK = 128
ROT_DIM = HEAD_DIM // 4
ROPE_THETA = 500000.0
DN_HEADS = D_MODEL // (2 * HEAD_DIM)
DN_HEAD_DIM = 128
DN_WIDTH = DN_HEADS * DN_HEAD_DIM
DN_CHUNK = 64
CONV_WIDTH = 4
NORM_EPS = 1e-6

SPLIT_SIZES = (
    NSA_WIDTH,
    6 * NSA_KV_WIDTH,
    3 * NSA_HEADS,
    NSA_WIDTH,
    3 * DN_WIDTH,
    DN_HEADS,
    DN_HEADS,
    DN_WIDTH,
    2 * D_MODEL,
)
SPLIT_POINTS = tuple(int(v) for v in np.cumsum(SPLIT_SIZES)[:-1])
N_IN = int(sum(SPLIT_SIZES))

kernel_name = "nsa_gated_deltanet_parallel_hybrid"


def rms_norm(x, g):
    xf = x.astype(jnp.float32)
    y = xf * lax.rsqrt(jnp.mean(xf * xf, axis=-1, keepdims=True) + NORM_EPS)
    return (y * g).astype(x.dtype)


def l2_norm(x):
    xf = x.astype(jnp.float32)
    return xf * lax.rsqrt(jnp.sum(xf * xf, axis=-1, keepdims=True) + NORM_EPS)


def rope_partial(x, pos):
    inv_freq = ROPE_THETA ** (-jnp.arange(0, ROT_DIM, 2, dtype=jnp.float32) / ROT_DIM)
    ang = pos.astype(jnp.float32)[:, None] * inv_freq[None, :]
    cos = jnp.cos(ang)[:, None, :]
    sin = jnp.sin(ang)[:, None, :]
    xr = x[..., :ROT_DIM].astype(jnp.float32)
    x1, x2 = xr[..., : ROT_DIM // 2], xr[..., ROT_DIM // 2:]
    rot = jnp.concatenate([x1 * cos - x2 * sin, x2 * cos + x1 * sin], axis=-1)
    return jnp.concatenate([rot.astype(x.dtype), x[..., ROT_DIM:]], axis=-1)


def masked_softmax(s, mask):
    s = jnp.where(mask, s, -jnp.inf)
    m = jnp.max(s, axis=-1, keepdims=True)
    m = jnp.where(jnp.isfinite(m), m, 0.0)
    p = jnp.exp(s - m)
    return p / jnp.maximum(jnp.sum(p, axis=-1, keepdims=True), jnp.finfo(jnp.float32).tiny)


def compress_blocks(t, pos_emb, w1, w2):
    S = t.shape[0]
    n_cmp = (S - CMP_BLOCK) // CMP_STRIDE + 1
    idx = jnp.arange(n_cmp)[:, None] * CMP_STRIDE + jnp.arange(CMP_BLOCK)[None, :]
    blocks = t[idx] + pos_emb[None, :, None, :]
    flat = blocks.transpose(0, 2, 1, 3).reshape(n_cmp, t.shape[1], CMP_BLOCK * t.shape[2])
    return jax.nn.silu(flat @ w1) @ w2


def nsa_sequence(q, k_cmp, v_cmp, k_sel, v_sel, k_win, v_win, gates, cmp_pos, cmp_w1, cmp_w2):
    S = q.shape[0]
    G, dh = NSA_KV_GROUPS, HEAD_DIM
    scale = HEAD_DIM ** -0.5
    kc = compress_blocks(k_cmp, cmp_pos[0], cmp_w1[0], cmp_w2[0])
    vc = compress_blocks(v_cmp, cmp_pos[1], cmp_w1[1], cmp_w2[1])
    n_cmp = kc.shape[0]
    n_sel = S // SEL_BLOCK
    n_top = min(SEL_TOPK, n_sel)
    cmp_start = jnp.arange(n_cmp) * CMP_STRIDE
    cmp_last = cmp_start + CMP_BLOCK - 1
    sel_start = jnp.arange(n_sel) * SEL_BLOCK
    overlap = ((cmp_start[:, None] < sel_start[None, :] + SEL_BLOCK)
               & (cmp_last[:, None] >= sel_start[None, :])).astype(jnp.float32)
    ks_g = k_sel.reshape(n_sel, SEL_BLOCK, G, dh).transpose(2, 0, 1, 3)
    vs_g = v_sel.reshape(n_sel, SEL_BLOCK, G, dh).transpose(2, 0, 1, 3)
    kw_pad = jnp.pad(k_win, ((WINDOW, 0), (0, 0), (0, 0)))
    vw_pad = jnp.pad(v_win, ((WINDOW, 0), (0, 0), (0, 0)))
    grp = jnp.arange(G)[:, None, None]
    blk = jnp.arange(n_sel)

    def query_block(b):
        s0 = b * Q_BLOCK
        qb = lax.dynamic_slice_in_dim(q, s0, Q_BLOCK, 0).reshape(Q_BLOCK, G, NSA_HPG, dh)
        gb = lax.dynamic_slice_in_dim(gates, s0, Q_BLOCK, 0)
        pos = s0 + jnp.arange(Q_BLOCK)
        s_c = jnp.einsum('tghd,cgd->ghtc', qb, kc, preferred_element_type=jnp.float32) * scale
        p_c = masked_softmax(s_c, cmp_last[None, :] <= pos[:, None])
        o_c = jnp.einsum('ghtc,cgd->tghd', p_c.astype(vc.dtype), vc)
        imp = jnp.einsum('ghtc,cj->gtj', p_c, overlap)
        cur = pos // SEL_BLOCK
        valid = sel_start[None, :] <= pos[:, None]
        forced = (blk[None, :] == 0) | ((blk[None, :] <= cur[:, None])
                                        & (blk[None, :] > cur[:, None] - SEL_LOCAL))
        score = jnp.where(valid, imp + jnp.where(forced, SEL_FORCE_SCORE, 0.0), -1.0)
        top_val, top_idx = lax.top_k(score, n_top)
        kg = ks_g[grp, top_idx].reshape(G, Q_BLOCK, n_top * SEL_BLOCK, dh)
        vg = vs_g[grp, top_idx].reshape(G, Q_BLOCK, n_top * SEL_BLOCK, dh)
        kpos = (top_idx[..., None] * SEL_BLOCK + jnp.arange(SEL_BLOCK)).reshape(G, Q_BLOCK, -1)
        kmask = (kpos <= pos[None, :, None]) & jnp.repeat(top_val >= 0.0, SEL_BLOCK, axis=-1)
        s_s = jnp.einsum('tghd,gtkd->ghtk', qb, kg, preferred_element_type=jnp.float32) * scale
        p_s = masked_softmax(s_s, kmask[:, None])
        o_s = jnp.einsum('ghtk,gtkd->tghd', p_s.astype(vg.dtype), vg)
        kwb = lax.dynamic_slice_in_dim(kw_pad, s0, Q_BLOCK + WINDOW, 0)
        vwb = lax.dynamic_slice_in_dim(vw_pad, s0, Q_BLOCK + WINDOW, 0)
        wpos = s0 - WINDOW + jnp.arange(Q_BLOCK + WINDOW)
        wmask = ((wpos[None, :] <= pos[:, None]) & (wpos[None, :] > pos[:, None] - WINDOW)
                 & (wpos[None, :] >= 0))
        s_w = jnp.einsum('tghd,kgd->ghtk', qb, kwb, preferred_element_type=jnp.float32) * scale
        p_w = masked_softmax(s_w, wmask)
        o_w = jnp.einsum('ghtk,kgd->tghd', p_w.astype(vwb.dtype), vwb)
        g = jax.nn.sigmoid(gb.astype(jnp.float32)).reshape(Q_BLOCK, G, NSA_HPG, 3)
        o = o_c * g[..., 0:1] + o_s * g[..., 1:2] + o_w * g[..., 2:3]
        return o.reshape(Q_BLOCK, NSA_HEADS, dh).astype(q.dtype)

    out = lax.map(query_block, jnp.arange(S // Q_BLOCK))
    return out.reshape(S, NSA_HEADS, dh)


def causal_depthwise_conv(x, w):
    K, C = w.shape
    return lax.conv_general_dilated(x, w[:, None, :], window_strides=(1,), padding=((K - 1, 0),),
                                    dimension_numbers=('NWC', 'WIO', 'NWC'), feature_group_count=C)


def gated_delta_rule(q, k, v, g, beta):
    B, S, H, dk = q.shape
    dv = v.shape[-1]
    C = DN_CHUNK
    N = S // C
    f32 = jnp.float32

    def to_chunks(t):
        return t.astype(f32).reshape(B, N, C, H, -1).transpose(0, 3, 1, 2, 4)

    qc, kc, vc = to_chunks(q), to_chunks(k), to_chunks(v)
    gc = g.astype(f32).reshape(B, N, C, H).transpose(0, 3, 1, 2)
    bc = beta.astype(f32).reshape(B, N, C, H).transpose(0, 3, 1, 2)
    Gc = jnp.cumsum(gc, axis=-1)
    causal = jnp.tril(jnp.ones((C, C), dtype=bool))
    strict = jnp.tril(jnp.ones((C, C), dtype=bool), k=-1)
    L = jnp.exp(jnp.where(causal, Gc[..., :, None] - Gc[..., None, :], -jnp.inf))
    kb = kc * bc[..., None]
    A = jnp.where(strict, jnp.einsum('bhnid,bhnjd->bhnij', kb, kc) * L, 0.0) + jnp.eye(C, dtype=f32)
    rhs = jnp.concatenate([vc * bc[..., None], kb * jnp.exp(Gc)[..., None]], axis=-1)
    sol = lax.linalg.triangular_solve(A, rhs, left_side=True, lower=True, unit_diagonal=True)
    u, w = sol[..., :dv], sol[..., dv:]
    attn = jnp.einsum('bhnid,bhnjd->bhnij', qc, kc) * L
    q_dec = qc * jnp.exp(Gc)[..., None]
    k_dec = kc * jnp.exp(Gc[..., -1:] - Gc)[..., None]
    g_last = jnp.exp(Gc[..., -1])

    def step(state, xs):
        u_n, w_n, attn_n, qd_n, kd_n, gl_n = xs
        v_new = u_n - jnp.einsum('bhcd,bhde->bhce', w_n, state)
        o = jnp.einsum('bhcd,bhde->bhce', qd_n, state) + jnp.einsum('bhij,bhje->bhie', attn_n, v_new)
        state = state * gl_n[..., None, None] + jnp.einsum('bhcd,bhce->bhde', kd_n, v_new)
        return state, o

    xs = tuple(jnp.moveaxis(t, 2, 0) for t in (u, w, attn, q_dec, k_dec, g_last))
    state0 = jnp.zeros((B, H, dk, dv), f32)
    _, o = lax.scan(step, state0, xs)
    return o.transpose(1, 0, 3, 2, 4).reshape(B, S, H, dv)


def hybrid_layer(x, norm_g, w_in, cmp_pos, cmp_w1, cmp_w2, dn_conv_w, dn_a_log, dn_dt_bias,
                 dn_out_norm_g, w_branch_nsa, w_branch_dn, w_out):
    B, S, _ = x.shape
    h = rms_norm(x, norm_g)
    proj = jnp.einsum('bsd,dn->bsn', h, w_in)
    (q, kv, nsa_gate, z_nsa, dn_qkv, dn_b, dn_a, z_dn, merge) = jnp.split(proj, SPLIT_POINTS, axis=-1)
    pos = jnp.arange(S)

    q = rope_partial(q.reshape(B, S, NSA_HEADS, HEAD_DIM), pos)
    kv = kv.reshape(B, S, 6, NSA_KV_GROUPS, HEAD_DIM)
    k_cmp, v_cmp, k_sel, v_sel, k_win, v_win = jnp.moveaxis(kv, 2, 0)
    k_cmp = rope_partial(k_cmp, pos)
    k_sel = rope_partial(k_sel, pos)
    k_win = rope_partial(k_win, pos)
    nsa_gate = nsa_gate.reshape(B, S, NSA_HEADS, 3)
    o_nsa = jax.vmap(nsa_sequence, in_axes=(0, 0, 0, 0, 0, 0, 0, 0, None, None, None))(
        q, k_cmp, v_cmp, k_sel, v_sel, k_win, v_win, nsa_gate, cmp_pos, cmp_w1, cmp_w2)
    y_nsa = (o_nsa.reshape(B, S, NSA_WIDTH) * jax.nn.silu(z_nsa)) @ w_branch_nsa

    dn_qkv = jax.nn.silu(causal_depthwise_conv(dn_qkv, dn_conv_w))
    dq, dk, dv = jnp.split(dn_qkv, 3, axis=-1)
    dq = l2_norm(dq.reshape(B, S, DN_HEADS, DN_HEAD_DIM)) * (DN_HEAD_DIM ** -0.5)
    dk = l2_norm(dk.reshape(B, S, DN_HEADS, DN_HEAD_DIM))
    dv = dv.reshape(B, S, DN_HEADS, DN_HEAD_DIM)
    beta = jax.nn.sigmoid(dn_b.astype(jnp.float32))
    g = -jnp.exp(dn_a_log.astype(jnp.float32)) * jax.nn.softplus(
        dn_a.astype(jnp.float32) + dn_dt_bias.astype(jnp.float32))
    o_dn = rms_norm(gated_delta_rule(dq, dk, dv, g, beta), dn_out_norm_g)
    y_dn = (o_dn.reshape(B, S, DN_WIDTH).astype(x.dtype) * jax.nn.silu(z_dn)) @ w_branch_dn

    g_nsa, g_dn = jnp.split(merge, 2, axis=-1)
    mixed = jax.nn.sigmoid(g_nsa) * y_nsa + jax.nn.sigmoid(g_dn) * y_dn
    return x + jnp.einsum('bsd,de->bse', mixed.astype(x.dtype), w_out)


def setup_inputs(seed: int = 0) -> dict:
    key = jax.random.key(seed)
    ks = jax.random.split(key, 16)
    f32 = jnp.float32

    def nrm(k, shape, fan_in):
        return jax.random.normal(k, shape, f32) * (fan_in ** -0.5)

    x = jax.random.normal(ks[0], (BATCH, SEQ, D_MODEL), f32)
    norm_g = 1.0 + 0.01 * jax.random.normal(ks[1], (DEPTH, D_MODEL), f32)
    w_in = nrm(ks[2], (DEPTH, D_MODEL, N_IN), D_MODEL)
    cmp_pos = 0.02 * jax.random.normal(ks[3], (DEPTH, 2, CMP_BLOCK, HEAD_DIM), f32)
    cmp_w1 = nrm(ks[4], (DEPTH, 2, CMP_BLOCK * HEAD_DIM, CMP_HIDDEN), CMP_BLOCK * HEAD_DIM)
    cmp_w2 = nrm(ks[5], (DEPTH, 2, CMP_HIDDEN, HEAD_DIM), CMP_HIDDEN)
    dn_conv_w = nrm(ks[6], (DEPTH, CONV_WIDTH, 3 * DN_WIDTH), CONV_WIDTH)
    dn_a_log = jnp.log(jax.random.uniform(ks[7], (DEPTH, DN_HEADS), f32, 1.0, 16.0))
    dt = jnp.exp(jax.random.uniform(ks[8], (DEPTH, DN_HEADS), f32, math_log(1e-3), math_log(1e-1)))
    dn_dt_bias = dt + jnp.log(-jnp.expm1(-dt))
    dn_out_norm_g = 1.0 + 0.01 * jax.random.normal(ks[9], (DEPTH, DN_HEAD_DIM), f32)
    w_branch_nsa = nrm(ks[10], (DEPTH, NSA_WIDTH, D_MODEL), NSA_WIDTH)
    w_branch_dn = nrm(ks[11], (DEPTH, DN_WIDTH, D_MODEL), DN_WIDTH)
    w_out = nrm(ks[12], (DEPTH, D_MODEL, D_MODEL), D_MODEL)
    final_norm_g = 1.0 + 0.01 * jax.random.normal(ks[13], (D_MODEL,), f32)
    return {"x": x, "norm_g": norm_g, "w_in": w_in, "cmp_pos": cmp_pos, "cmp_w1": cmp_w1,
            "cmp_w2": cmp_w2, "dn_conv_w": dn_conv_w, "dn_a_log": dn_a_log,
            "dn_dt_bias": dn_dt_bias, "dn_out_norm_g": dn_out_norm_g,
            "w_branch_nsa": w_branch_nsa, "w_branch_dn": w_branch_dn, "w_out": w_out,
            "final_norm_g": final_norm_g}


def math_log(v):
    return float(np.log(v))


def reference(x, norm_g, w_in, cmp_pos, cmp_w1, cmp_w2, dn_conv_w, dn_a_log, dn_dt_bias,
              dn_out_norm_g, w_branch_nsa, w_branch_dn, w_out, final_norm_g):
    for l in range(DEPTH):
        x = hybrid_layer(x, norm_g[l], w_in[l], cmp_pos[l], cmp_w1[l], cmp_w2[l], dn_conv_w[l],
                         dn_a_log[l], dn_dt_bias[l], dn_out_norm_g[l], w_branch_nsa[l],
                         w_branch_dn[l], w_out[l])
    return rms_norm(x, final_norm_g)
```

```python
import functools
import math

import numpy as np
import jax
import jax.numpy as jnp
from jax import lax
from jax.experimental import pallas as pl
from jax.experimental.pallas import tpu as pltpu

D_MODEL = 4096
HEAD_DIM = 128
NSA_HEADS = 16
NSA_GROUPS = 4
NSA_HPG = 4
NSA_WIDTH = NSA_HEADS * HEAD_DIM
NSA_KV_WIDTH = NSA_GROUPS * HEAD_DIM
CMP_BLOCK = 32
CMP_STRIDE = 16
CMP_HIDDEN = 256
SEL_BLOCK = 64
SEL_TOPK = 16
SEL_LOCAL = 2
SEL_FORCE_SCORE = 1e4
WINDOW = 512
ROT_DIM = 32
ROPE_THETA = 500000.0
DN_HEADS = 16
DN_WIDTH = DN_HEADS * HEAD_DIM
DN_CHUNK = 64
CONV_WIDTH = 4
NORM_EPS = 1e-6
ATTN_SCALE = HEAD_DIM ** -0.5

LANES = 128
SUBLANES = 8
VMEM_LIMIT_BYTES = 56 * 1024 * 1024

NEG_BIG = -1e30
BF16 = jnp.bfloat16
F32 = jnp.float32


def _cparams(*sem):
    return pltpu.CompilerParams(dimension_semantics=sem, vmem_limit_bytes=VMEM_LIMIT_BYTES)


def _dot(a, b):
    return jnp.dot(a.astype(BF16), b.astype(BF16), preferred_element_type=F32)


def _dot_nt(a, b):
    return lax.dot_general(a.astype(BF16), b.astype(BF16), (((1,), (1,)), ((), ())),
                           preferred_element_type=F32)


def _silu(x):
    return x * jax.nn.sigmoid(x)


def _softplus(x):
    return jnp.maximum(x, 0.0) + jnp.log1p(jnp.exp(-jnp.abs(x)))


def _rmsnorm_kernel(x_ref, g_ref, o_ref):
    x = x_ref[...]
    ms = jnp.mean(x * x, axis=-1, keepdims=True)
    o_ref[...] = (x * lax.rsqrt(ms + NORM_EPS) * g_ref[...]).astype(o_ref.dtype)


def rmsnorm(x, g, out_dtype, rows=256):
    S, D = x.shape
    rows = min(rows, S)
    return pl.pallas_call(
        _rmsnorm_kernel,
        grid=(S // rows,),
        in_specs=[pl.BlockSpec((rows, D), lambda i: (i, 0)),
                  pl.BlockSpec((1, D), lambda i: (0, 0))],
        out_specs=pl.BlockSpec((rows, D), lambda i: (i, 0)),
        out_shape=jax.ShapeDtypeStruct((S, D), out_dtype),
        compiler_params=_cparams("parallel"),
        name="rmsnorm",
    )(x, g.reshape(1, D))


def _mm_kernel(a_ref, b_ref, o_ref):
    o_ref[...] = jnp.dot(a_ref[...], b_ref[...], preferred_element_type=F32).astype(o_ref.dtype)


def _rope_tile(x, c, s1, s2):
    return x * c + pltpu.roll(x, LANES - ROT_DIM // 2, 1) * s1 + pltpu.roll(x, ROT_DIM // 2, 1) * s2


def _mm_rope_kernel(a_ref, b_ref, c_ref, s1_ref, s2_ref, o_ref):
    acc = jnp.dot(a_ref[...], b_ref[...], preferred_element_type=F32)
    c, s1, s2 = c_ref[...], s1_ref[...], s2_ref[...]
    for h in range(acc.shape[1] // LANES):
        sl = slice(h * LANES, (h + 1) * LANES)
        o_ref[:, sl] = _rope_tile(acc[:, sl], c, s1, s2).astype(o_ref.dtype)


def _mm_tiles(M, N, tm, tn):
    tm = min(tm, M)
    tn = min(tn, N)
    assert M % tm == 0 and N % tn == 0, (M, N, tm, tn)
    return tm, tn


def matmul(a, b, out_dtype, tm=1024, tn=512, name="matmul"):
    M, K = a.shape
    N = b.shape[1]
    tm, tn = _mm_tiles(M, N, tm, tn)
    return pl.pallas_call(
        _mm_kernel,
        grid=(M // tm, N // tn),
        in_specs=[pl.BlockSpec((tm, K), lambda i, j: (i, 0)),
                  pl.BlockSpec((K, tn), lambda i, j: (0, j))],
        out_specs=pl.BlockSpec((tm, tn), lambda i, j: (i, j)),
        out_shape=jax.ShapeDtypeStruct((M, N), out_dtype),
        compiler_params=_cparams("parallel", "arbitrary"),
        name=name,
    )(a, b)


def matmul_rope(a, b, rope_tabs, out_dtype, tm=1024, tn=512, name="matmul_rope"):
    M, K = a.shape
    N = b.shape[1]
    tm, tn = _mm_tiles(M, N, tm, tn)
    tab_spec = pl.BlockSpec((tm, LANES), lambda i, j: (i, 0))
    return pl.pallas_call(
        _mm_rope_kernel,
        grid=(M // tm, N // tn),
        in_specs=[pl.BlockSpec((tm, K), lambda i, j: (i, 0)),
                  pl.BlockSpec((K, tn), lambda i, j: (0, j)),
                  tab_spec, tab_spec, tab_spec],
        out_specs=pl.BlockSpec((tm, tn), lambda i, j: (i, j)),
        out_shape=jax.ShapeDtypeStruct((M, N), out_dtype),
        compiler_params=_cparams("parallel", "arbitrary"),
        name=name,
    )(a, b, *rope_tabs)


def _mm_merge_kernel(a1_ref, w1_ref, a2_ref, w2_ref, g1_ref, g2_ref, o_ref):
    y1 = jnp.dot(a1_ref[...], w1_ref[...], preferred_element_type=F32)
    y2 = jnp.dot(a2_ref[...], w2_ref[...], preferred_element_type=F32)
    o_ref[...] = (jax.nn.sigmoid(g1_ref[...]) * y1 + jax.nn.sigmoid(g2_ref[...]) * y2).astype(o_ref.dtype)


def matmul_merge(a1, w1, a2, w2, merge, out_dtype, tm=1024, tn=512):
    M, K = a1.shape
    N = w1.shape[1]
    tm, tn = _mm_tiles(M, N, tm, tn)
    nj = N // tn
    return pl.pallas_call(
        _mm_merge_kernel,
        grid=(M // tm, nj),
        in_specs=[pl.BlockSpec((tm, K), lambda i, j: (i, 0)),
                  pl.BlockSpec((K, tn), lambda i, j: (0, j)),
                  pl.BlockSpec((tm, K), lambda i, j: (i, 0)),
                  pl.BlockSpec((K, tn), lambda i, j: (0, j)),
                  pl.BlockSpec((tm, tn), lambda i, j: (i, j)),
                  pl.BlockSpec((tm, tn), lambda i, j: (i, j + nj))],
        out_specs=pl.BlockSpec((tm, tn), lambda i, j: (i, j)),
        out_shape=jax.ShapeDtypeStruct((M, N), out_dtype),
        compiler_params=_cparams("parallel", "arbitrary"),
        name="matmul_merge",
    )(a1, w1, a2, w2, merge, merge)


def _mm_resid_kernel(a_ref, b_ref, x_ref, o_ref):
    o_ref[...] = x_ref[...] + jnp.dot(a_ref[...], b_ref[...], preferred_element_type=F32)


def matmul_resid(a, b, x, tm=1024, tn=512):
    M, K = a.shape
    N = b.shape[1]
    tm, tn = _mm_tiles(M, N, tm, tn)
    return pl.pallas_call(
        _mm_resid_kernel,
        grid=(M // tm, N // tn),
        in_specs=[pl.BlockSpec((tm, K), lambda i, j: (i, 0)),
                  pl.BlockSpec((K, tn), lambda i, j: (0, j)),
                  pl.BlockSpec((tm, tn), lambda i, j: (i, j))],
        out_specs=pl.BlockSpec((tm, tn), lambda i, j: (i, j)),
        out_shape=jax.ShapeDtypeStruct((M, N), F32),
        compiler_params=_cparams("parallel", "arbitrary"),
        name="matmul_resid",
    )(a, b, x)


def _compress_kernel(x_ref, w1_ref, pos_ref, w2_ref, o_ref):
    ncp = x_ref.shape[2]
    uv = jnp.dot(x_ref[0, 0], w1_ref[0], preferred_element_type=F32)
    bias = jnp.dot(pos_ref[0], w1_ref[0], preferred_element_type=F32)
    top = uv[:, :CMP_HIDDEN] + bias[0:1, :CMP_HIDDEN]
    bot = uv[:, CMP_HIDDEN:]
    bias_bot = bias[1:2, CMP_HIDDEN:]
    bot_next = pltpu.roll(bot, ncp - 1, 0)
    hid = _silu(top + bot_next + bias_bot)
    o_ref[0, 0] = jnp.dot(hid.astype(BF16), w2_ref[0], preferred_element_type=F32).astype(o_ref.dtype)


def compress(xg, w1cat, pos2, w2):
    _, G, ncp, _ = xg.shape
    half = CMP_STRIDE * HEAD_DIM
    return pl.pallas_call(
        _compress_kernel,
        grid=(2, G),
        in_specs=[pl.BlockSpec((1, 1, ncp, half), lambda a, g: (a, g, 0, 0)),
                  pl.BlockSpec((1, half, 2 * CMP_HIDDEN), lambda a, g: (a, 0, 0)),
                  pl.BlockSpec((1, SUBLANES, half), lambda a, g: (a, 0, 0)),
                  pl.BlockSpec((1, CMP_HIDDEN, HEAD_DIM), lambda a, g: (a, 0, 0))],
        out_specs=pl.BlockSpec((1, 1, ncp, HEAD_DIM), lambda a, g: (a, g, 0, 0)),
        out_shape=jax.ShapeDtypeStruct((2, G, ncp, HEAD_DIM), BF16),
        compiler_params=_cparams("parallel", "parallel"),
        name="nsa_compress",
    )(xg, w1cat, pos2, w2)


def _heads_on_lanes(ref, tq):
    return jnp.concatenate([ref[h * HEAD_DIM:(h + 1) * HEAD_DIM, :] for h in range(NSA_HPG)], axis=1)


def _cmp_select_kernel(qT_ref, kc_ref, vcT_ref, ovT_ref, ocT_ref, selT_ref, *, tq, n_cmp, n_top):
    qi = pl.program_id(1)
    ncp = kc_ref.shape[1]
    n_sel = ovT_ref.shape[0]
    q4 = _heads_on_lanes(qT_ref, tq)
    sT = jnp.dot(kc_ref[0], q4, preferred_element_type=F32) * ATTN_SCALE
    pos = qi * tq + lax.broadcasted_iota(jnp.int32, (1, tq), 1)
    c_idx = lax.broadcasted_iota(jnp.int32, (ncp, tq), 0)
    valid = (c_idx * CMP_STRIDE + (CMP_BLOCK - 1) <= pos) & (c_idx < n_cmp)
    valid4 = jnp.concatenate([valid] * NSA_HPG, axis=1)
    sm = jnp.where(valid4, sT, NEG_BIG)
    m = jnp.max(sm, axis=0, keepdims=True)
    p = jnp.where(valid4, jnp.exp(sm - m), 0.0)
    l = jnp.sum(p, axis=0, keepdims=True)
    pn = p / jnp.maximum(l, jnp.finfo(jnp.float32).tiny)
    ocT = jnp.dot(vcT_ref[0], pn.astype(BF16), preferred_element_type=F32)
    for h in range(NSA_HPG):
        ocT_ref[h * HEAD_DIM:(h + 1) * HEAD_DIM, :] = ocT[:, h * tq:(h + 1) * tq]
    psum = pn[:, 0:tq]
    for h in range(1, NSA_HPG):
        psum = psum + pn[:, h * tq:(h + 1) * tq]
    p_hi = psum.astype(BF16)
    p_lo = (psum - p_hi.astype(F32)).astype(BF16)
    ovT = ovT_ref[...]
    imp = (jnp.dot(ovT, p_hi, preferred_element_type=F32)
           + jnp.dot(ovT, p_lo, preferred_element_type=F32))
    j_idx = lax.broadcasted_iota(jnp.int32, (n_sel, tq), 0)
    cur = pos // SEL_BLOCK
    sel_valid = j_idx * SEL_BLOCK <= pos
    forced = (j_idx == 0) | ((j_idx <= cur) & (j_idx > cur - SEL_LOCAL))
    work = jnp.where(sel_valid, imp + jnp.where(forced, SEL_FORCE_SCORE, 0.0), -1.0)
    jf = j_idx.astype(F32)
    sel = jnp.zeros((n_sel, tq), F32)
    for _ in range(n_top):
        mx = jnp.max(work, axis=0, keepdims=True)
        first = jnp.min(jnp.where(work == mx, jf, float(n_sel)), axis=0, keepdims=True)
        hit = jf == first
        sel = jnp.where(hit & (mx >= 0.0), 1.0, sel)
        work = jnp.where(hit, -2.0, work)
    selT_ref[0] = sel


def cmp_select(qT, kc, vcT, ovT, S, tq=128):
    G, ncp, _ = kc.shape
    n_sel = S // SEL_BLOCK
    n_cmp = (S - CMP_BLOCK) // CMP_STRIDE + 1
    n_top = min(SEL_TOPK, n_sel)
    kern = functools.partial(_cmp_select_kernel, tq=tq, n_cmp=n_cmp, n_top=n_top)
    return pl.pallas_call(
        kern,
        grid=(G, S // tq),
        in_specs=[pl.BlockSpec((NSA_HPG * HEAD_DIM, tq), lambda g, i: (g, i)),
                  pl.BlockSpec((1, ncp, HEAD_DIM), lambda g, i: (g, 0, 0)),
                  pl.BlockSpec((1, HEAD_DIM, ncp), lambda g, i: (g, 0, 0)),
                  pl.BlockSpec((n_sel, ncp), lambda g, i: (0, 0))],
        out_specs=[pl.BlockSpec((NSA_HPG * HEAD_DIM, tq), lambda g, i: (g, i)),
                   pl.BlockSpec((1, n_sel, tq), lambda g, i: (g, 0, i))],
        out_shape=[jax.ShapeDtypeStruct((NSA_WIDTH, S), F32),
                   jax.ShapeDtypeStruct((G, n_sel, S), F32)],
        compiler_params=_cparams("parallel", "parallel"),
        name="nsa_cmp_select",
    )(qT, kc, vcT, ovT)


def _nsa_main_kernel(qT_ref, ksel_ref, vselT_ref, selT_ref, kwin_ref, vwinT_ref, ocT_ref,
                     gate_ref, z_ref, o_ref, acc_ref, m_ref, l_ref, *, tq, kt):
    qi = pl.program_id(1)
    s0 = qi * tq
    q4 = _heads_on_lanes(qT_ref, tq)
    tpos = s0 + lax.broadcasted_iota(jnp.int32, (1, tq), 1)
    bpt = kt // SEL_BLOCK

    acc_ref[...] = jnp.zeros_like(acc_ref)
    m_ref[...] = jnp.full_like(m_ref, NEG_BIG)
    l_ref[...] = jnp.zeros_like(l_ref)

    def sel_tile(i, carry):
        k_t = ksel_ref[pl.ds(pl.multiple_of(i * kt, kt), kt), :]
        sT = jnp.dot(k_t, q4, preferred_element_type=F32) * ATTN_SCALE
        sel8 = selT_ref[0, pl.ds(pl.multiple_of(i * bpt, bpt), bpt), :]
        selx = jnp.concatenate(
            [jnp.broadcast_to(sel8[r:r + 1, :], (SEL_BLOCK, tq)) for r in range(bpt)], axis=0)
        kpos = i * kt + lax.broadcasted_iota(jnp.int32, (kt, tq), 0)
        ok = (selx > 0.5) & (kpos <= tpos)
        bias = jnp.where(ok, 0.0, NEG_BIG)
        sm = sT + jnp.concatenate([bias] * NSA_HPG, axis=1)
        m_old = m_ref[...]
        m_new = jnp.maximum(m_old, jnp.max(sm, axis=0, keepdims=True))
        alpha = jnp.exp(m_old - m_new)
        p = jnp.exp(sm - m_new)
        l_ref[...] = l_ref[...] * alpha + jnp.sum(p, axis=0, keepdims=True)
        acc_ref[...] = acc_ref[...] * alpha + jnp.dot(vselT_ref[0, i], p.astype(BF16),
                                                      preferred_element_type=F32)
        m_ref[...] = m_new
        return carry

    n_tiles = (s0 + tq + kt - 1) // kt
    lax.fori_loop(0, n_tiles, sel_tile, 0)
    o_sel = acc_ref[...] / l_ref[...]

    n_wt = (WINDOW + tq) // LANES
    t0 = jnp.maximum(s0 // LANES - WINDOW // LANES, 0)
    kw = kwin_ref[pl.ds(pl.multiple_of(t0 * LANES, LANES), WINDOW + tq), :]
    sw = jnp.dot(kw, q4, preferred_element_type=F32) * ATTN_SCALE
    wpos = t0 * LANES + lax.broadcasted_iota(jnp.int32, (WINDOW + tq, tq), 0)
    okw = (wpos <= tpos) & (wpos > tpos - WINDOW)
    biasw = jnp.where(okw, 0.0, NEG_BIG)
    sw = sw + jnp.concatenate([biasw] * NSA_HPG, axis=1)
    mw = jnp.max(sw, axis=0, keepdims=True)
    pw = jnp.exp(sw - mw)
    lw = jnp.sum(pw, axis=0, keepdims=True)
    vwT = jnp.concatenate([vwinT_ref[0, t0 + j] for j in range(n_wt)], axis=1)
    o_win = jnp.dot(vwT, pw.astype(BF16), preferred_element_type=F32) / lw

    for h in range(NSA_HPG):
        hs = slice(h * tq, (h + 1) * tq)
        g0 = jax.nn.sigmoid(gate_ref[0, 0, h:h + 1, :])
        g1 = jax.nn.sigmoid(gate_ref[1, 0, h:h + 1, :])
        g2 = jax.nn.sigmoid(gate_ref[2, 0, h:h + 1, :])
        oT = ocT_ref[h * HEAD_DIM:(h + 1) * HEAD_DIM, :] * g0 + o_sel[:, hs] * g1 + o_win[:, hs] * g2
        cs = slice(h * HEAD_DIM, (h + 1) * HEAD_DIM)
        o_ref[:, cs] = (oT.T * _silu(z_ref[:, cs])).astype(o_ref.dtype)


def nsa_main(qT, k_all, vselT, selT, vwinT, ocT, gatesT, z, S, tq=128, kt=512):
    G = NSA_GROUPS
    kt = min(kt, S)
    n_sel = S // SEL_BLOCK
    assert tq == LANES and S % kt == 0 and S >= WINDOW + tq
    kern = functools.partial(_nsa_main_kernel, tq=tq, kt=kt)
    gw = NSA_HPG * HEAD_DIM
    return pl.pallas_call(
        kern,
        grid=(G, S // tq),
        in_specs=[pl.BlockSpec((gw, tq), lambda g, i: (g, i)),
                  pl.BlockSpec((S, HEAD_DIM), lambda g, i: (0, G + g)),
                  pl.BlockSpec((1, S // kt, HEAD_DIM, kt), lambda g, i: (g, 0, 0, 0)),
                  pl.BlockSpec((1, n_sel, tq), lambda g, i: (g, 0, i)),
                  pl.BlockSpec((S, HEAD_DIM), lambda g, i: (0, 2 * G + g)),
                  pl.BlockSpec((1, S // LANES, HEAD_DIM, LANES), lambda g, i: (g, 0, 0, 0)),
                  pl.BlockSpec((gw, tq), lambda g, i: (g, i)),
                  pl.BlockSpec((3, 1, SUBLANES, tq), lambda g, i: (0, g, 0, i)),
                  pl.BlockSpec((tq, gw), lambda g, i: (i, g))],
        out_specs=pl.BlockSpec((tq, gw), lambda g, i: (i, g)),
        out_shape=jax.ShapeDtypeStruct((S, NSA_WIDTH), BF16),
        scratch_shapes=[pltpu.VMEM((HEAD_DIM, NSA_HPG * tq), F32),
                        pltpu.VMEM((1, NSA_HPG * tq), F32),
                        pltpu.VMEM((1, NSA_HPG * tq), F32)],
        compiler_params=_cparams("parallel", "arbitrary"),
        name="nsa_main",
    )(qT, k_all, vselT, selT, k_all, vwinT, ocT, gatesT, z)


DN_PAIR = 2 * DN_CHUNK


def _lane_pick(x, lane):
    li = lax.broadcasted_iota(jnp.int32, x.shape, 1)
    col = jnp.sum(jnp.where(li == lane, x, 0.0), axis=1, keepdims=True)
    return jnp.broadcast_to(col, x.shape)


def _conv_silu(cur_ref, halo_ref, w_ref, buf_ref, hs, first):
    rows = cur_ref.shape[0]
    halo = halo_ref[:, hs]
    buf_ref[0:SUBLANES, :] = jnp.where(first, jnp.zeros_like(halo), halo)
    buf_ref[SUBLANES:SUBLANES + rows, :] = cur_ref[:, hs]
    y = jnp.zeros((rows, LANES), F32)
    for k in range(CONV_WIDTH):
        off = SUBLANES - (CONV_WIDTH - 1) + k
        y = y + buf_ref[off:off + rows, :] * w_ref[k:k + 1, hs]
    return _silu(y)


def _dn_prep_kernel(q_ref, qh_ref, k_ref, kh_ref, v_ref, vh_ref, wq_ref, wk_ref, wv_ref,
                    small_ref, alog_ref, dtb_ref,
                    u_ref, wqd_ref, ak_ref, gl_ref, buf_ref, *, hb):
    p = pl.program_id(0)
    hblk = pl.program_id(1)
    first = p == 0
    R = DN_PAIR
    ri = lax.broadcasted_iota(jnp.int32, (R, R), 0)
    ci = lax.broadcasted_iota(jnp.int32, (R, R), 1)
    same = (ri // DN_CHUNK) == (ci // DN_CHUNK)
    causal = same & (ci <= ri)
    strict = same & (ci < ri)
    eye = (ri == ci).astype(F32)
    rin = ri % DN_CHUNK

    small = small_ref[...]
    beta_all = jax.nn.sigmoid(small)
    g_all = -jnp.exp(alog_ref[...]) * _softplus(small + dtb_ref[...])

    for hh in range(hb):
        hs = slice(hh * LANES, (hh + 1) * LANES)
        head = hblk * hb + hh
        q = _conv_silu(q_ref, qh_ref, wq_ref, buf_ref, hs, first)
        k = _conv_silu(k_ref, kh_ref, wk_ref, buf_ref, hs, first)
        v = _conv_silu(v_ref, vh_ref, wv_ref, buf_ref, hs, first)
        q = q * lax.rsqrt(jnp.sum(q * q, axis=1, keepdims=True) + NORM_EPS) * (HEAD_DIM ** -0.5)
        k = k * lax.rsqrt(jnp.sum(k * k, axis=1, keepdims=True) + NORM_EPS)
        beta = _lane_pick(beta_all, 3 * NSA_HEADS + head)
        gc = _lane_pick(g_all, 3 * NSA_HEADS + DN_HEADS + head)
        sh = 1
        while sh < DN_CHUNK:
            gc = gc + jnp.where(rin >= sh, pltpu.roll(gc, sh, 0), 0.0)
            sh *= 2
        g_last = jnp.concatenate(
            [jnp.broadcast_to(gc[DN_CHUNK - 1:DN_CHUNK, :], (DN_CHUNK, LANES)),
             jnp.broadcast_to(gc[R - 1:R, :], (DN_CHUNK, LANES))], axis=0)
        e_g = jnp.exp(gc)
        decay = jnp.exp(jnp.where(causal, gc - gc.T, -jnp.inf))
        kb = k * beta
        a = jnp.where(strict, _dot_nt(kb, k) * decay, 0.0)
        x = eye - a
        ap = _dot(a, a)
        n_sq = int(math.log2(DN_CHUNK)) - 1
        for s in range(n_sq):
            x = x + _dot(x, ap)
            if s + 1 < n_sq:
                ap = _dot(ap, ap)
        u = _dot(x, v * beta)
        w = _dot(x, kb * e_g)
        attn = jnp.where(causal, _dot_nt(q, k) * decay, 0.0)
        qd = q * e_g
        kdT = (k * jnp.exp(g_last - gc)).T
        first_half = ci < DN_CHUNK
        u_ref[:, hs] = u
        wqd_ref[:, hs] = jnp.concatenate(
            [w[:DN_CHUNK], qd[:DN_CHUNK], w[DN_CHUNK:], qd[DN_CHUNK:]], axis=0).astype(BF16)
        ak_ref[:, hs] = jnp.concatenate(
            [attn[:DN_CHUNK], jnp.where(first_half, kdT, 0.0),
             attn[DN_CHUNK:], jnp.where(first_half, 0.0, kdT)], axis=0).astype(BF16)
        r8 = lax.broadcasted_iota(jnp.int32, (SUBLANES, LANES), 0)
        gl = jnp.where(r8 == 0, jnp.exp(g_last[0:SUBLANES]),
                       jnp.where(r8 == 1, jnp.exp(g_last[DN_CHUNK:DN_CHUNK + SUBLANES]), 0.0))
        gl_ref[0, hh] = gl


def dn_prep(dnqkv, conv_w, small, alog_row, dtb_row, S, hb=4):
    R = DN_PAIR
    NP = S // R
    H = DN_HEADS
    nhb = H // hb
    bw = hb * LANES
    rpb = R // SUBLANES

    def cur(off):
        return pl.BlockSpec((R, bw), lambda p, h: (p, off * nhb + h))

    def halo(off):
        return pl.BlockSpec((SUBLANES, bw), lambda p, h: (jnp.maximum(p * rpb - 1, 0), off * nhb + h))

    def wspec(off):
        return pl.BlockSpec((CONV_WIDTH, bw), lambda p, h: (0, off * nhb + h))

    row = pl.BlockSpec((1, LANES), lambda p, h: (0, 0))
    kern = functools.partial(_dn_prep_kernel, hb=hb)
    return pl.pallas_call(
        kern,
        grid=(NP, nhb),
        in_specs=[cur(0), halo(0), cur(1), halo(1), cur(2), halo(2), wspec(0), wspec(1), wspec(2),
                  pl.BlockSpec((R, LANES), lambda p, h: (p, 0)), row, row],
        out_specs=[pl.BlockSpec((R, bw), lambda p, h: (p, h)),
                   pl.BlockSpec((2 * R, bw), lambda p, h: (p, h)),
                   pl.BlockSpec((3 * R, bw), lambda p, h: (p, h)),
                   pl.BlockSpec((1, hb, SUBLANES, LANES), lambda p, h: (p, h, 0, 0))],
        out_shape=[jax.ShapeDtypeStruct((S, DN_WIDTH), F32),
                   jax.ShapeDtypeStruct((2 * S, DN_WIDTH), BF16),
                   jax.ShapeDtypeStruct((3 * S, DN_WIDTH), BF16),
                   jax.ShapeDtypeStruct((NP, H, SUBLANES, LANES), F32)],
        scratch_shapes=[pltpu.VMEM((SUBLANES + R, LANES), F32)],
        compiler_params=_cparams("parallel", "parallel"),
        name="dn_prep",
    )(dnqkv, dnqkv, dnqkv, dnqkv, dnqkv, dnqkv, conv_w, conv_w, conv_w, small, alog_row, dtb_row)


def _dn_scan_kernel(u_ref, wqd_ref, ak_ref, gl_ref, z_ref, ng_ref, o_ref, state_ref):
    p = pl.program_id(0)

    @pl.when(p == 0)
    def _():
        state_ref[...] = jnp.zeros_like(state_ref)

    C = DN_CHUNK
    zeros = jnp.zeros((C, LANES), F32)
    ng = ng_ref[...]
    for h in range(DN_HEADS):
        hs = slice(h * LANES, (h + 1) * LANES)
        st = state_ref[h]
        outs = []
        for c in range(2):
            wqd = wqd_ref[2 * c * C:2 * (c + 1) * C, hs]
            r1 = jnp.dot(wqd, st.astype(BF16), preferred_element_type=F32)
            vn = u_ref[c * C:(c + 1) * C, hs] - r1[:C]
            vn_full = jnp.concatenate([vn, zeros] if c == 0 else [zeros, vn], axis=0)
            ak = ak_ref[3 * c * C:3 * (c + 1) * C, hs]
            r2 = jnp.dot(ak, vn_full.astype(BF16), preferred_element_type=F32)
            outs.append(r1[C:] + r2[:C])
            st = st * gl_ref[0, h, c:c + 1, :] + r2[C:]
        state_ref[h] = st
        o = jnp.concatenate(outs, axis=0)
        o = o * lax.rsqrt(jnp.mean(o * o, axis=1, keepdims=True) + NORM_EPS) * ng
        o_ref[:, hs] = (o * _silu(z_ref[:, hs])).astype(o_ref.dtype)


def dn_scan(u, wqd, ak, gl, z_dn, norm_g, S):
    R = DN_PAIR
    NP = S // R
    W = DN_WIDTH
    return pl.pallas_call(
        _dn_scan_kernel,
        grid=(NP,),
        in_specs=[pl.BlockSpec((R, W), lambda p: (p, 0)),
                  pl.BlockSpec((2 * R, W), lambda p: (p, 0)),
                  pl.BlockSpec((3 * R, W), lambda p: (p, 0)),
                  pl.BlockSpec((1, DN_HEADS, SUBLANES, LANES), lambda p: (p, 0, 0, 0)),
                  pl.BlockSpec((R, W), lambda p: (p, 0)),
                  pl.BlockSpec((1, LANES), lambda p: (0, 0))],
        out_specs=pl.BlockSpec((R, W), lambda p: (p, 0)),
        out_shape=jax.ShapeDtypeStruct((S, W), BF16),
        scratch_shapes=[pltpu.VMEM((DN_HEADS, HEAD_DIM, HEAD_DIM), F32)],
        compiler_params=_cparams("arbitrary"),
        name="dn_scan",
    )(u, wqd, ak, gl, z_dn, norm_g.reshape(1, LANES))


def _rope_tables(S):
    inv_freq = ROPE_THETA ** (-jnp.arange(0, ROT_DIM, 2, dtype=F32) / ROT_DIM)
    ang = jnp.arange(S).astype(F32)[:, None] * inv_freq[None, :]
    cos, sin = jnp.cos(ang), jnp.sin(ang)
    half = ROT_DIM // 2
    rest = LANES - ROT_DIM
    c = jnp.concatenate([cos, cos, jnp.ones((S, rest), F32)], axis=1)
    s1 = jnp.concatenate([-sin, jnp.zeros((S, LANES - half), F32)], axis=1)
    s2 = jnp.concatenate([jnp.zeros((S, half), F32), sin, jnp.zeros((S, rest), F32)], axis=1)
    return c, s1, s2


def _overlap_T(S):
    n_cmp = (S - CMP_BLOCK) // CMP_STRIDE + 1
    ncp = S // CMP_STRIDE
    n_sel = S // SEL_BLOCK
    c = np.arange(ncp)[None, :]
    j = np.arange(n_sel)[:, None]
    ov = (c * CMP_STRIDE < j * SEL_BLOCK + SEL_BLOCK) & (c * CMP_STRIDE + CMP_BLOCK - 1 >= j * SEL_BLOCK) & (c < n_cmp)
    return jnp.asarray(ov, dtype=BF16)


_SEG = np.cumsum([0, NSA_WIDTH, 6 * NSA_KV_WIDTH, 3 * NSA_HEADS, NSA_WIDTH, 3 * DN_WIDTH,
                  DN_HEADS, DN_HEADS, DN_WIDTH, 2 * D_MODEL])


def _split_w_in(w):
    seg = lambda i: w[:, _SEG[i]:_SEG[i + 1]]
    kv = seg(1).reshape(D_MODEL, 6, NSA_KV_WIDTH)
    wk = kv[:, 0::2].reshape(D_MODEL, 3 * NSA_KV_WIDTH)
    wv = kv[:, 1::2].reshape(D_MODEL, 3 * NSA_KV_WIDTH)
    n_small = 3 * NSA_HEADS + 2 * DN_HEADS
    small = jnp.concatenate([seg(2), seg(5), seg(6), jnp.zeros((D_MODEL, LANES - n_small), w.dtype)], axis=1)
    parts = dict(q=seg(0), k=wk, v=wv, small=small, z_nsa=seg(3), dn_qkv=seg(4), z_dn=seg(7), merge=seg(8))
    return {n: p.astype(BF16) for n, p in parts.items()}


def _layer(x, norm_g, w_in, cmp_pos, cmp_w1, cmp_w2, dn_conv_w, dn_a_log, dn_dt_bias,
           dn_out_norm_g, w_branch_nsa, w_branch_dn, w_out, consts):
    S = x.shape[0]
    G = NSA_GROUPS
    rope_tabs, ovT = consts
    wp = _split_w_in(w_in)
    h = rmsnorm(x, norm_g, BF16)

    q = matmul_rope(h, wp["q"], rope_tabs, BF16, name="proj_q")
    k_all = matmul_rope(h, wp["k"], rope_tabs, BF16, name="proj_k")
    v_all = matmul(h, wp["v"], BF16, name="proj_v")
    small = matmul(h, wp["small"], F32, tn=LANES, name="proj_small")
    z_nsa = matmul(h, wp["z_nsa"], F32, name="proj_z_nsa")
    dnqkv = matmul(h, wp["dn_qkv"], F32, name="proj_dn_qkv")
    z_dn = matmul(h, wp["z_dn"], F32, name="proj_z_dn")
    merge = matmul(h, wp["merge"], F32, name="proj_merge")

    ncp = S // CMP_STRIDE
    kvw = NSA_KV_WIDTH

    def chunked(t):
        return t.reshape(ncp, CMP_STRIDE, G, HEAD_DIM).transpose(2, 0, 1, 3).reshape(G, ncp, CMP_STRIDE * HEAD_DIM)

    xg = jnp.stack([chunked(k_all[:, :kvw]), chunked(v_all[:, :kvw])])
    half = CMP_STRIDE * HEAD_DIM
    w1cat = jnp.concatenate([cmp_w1[:, :half], cmp_w1[:, half:]], axis=2).astype(BF16)
    pos_flat = cmp_pos.reshape(2, 2, half)
    pos2 = jnp.concatenate([pos_flat, jnp.zeros((2, SUBLANES - 2, half), F32)], axis=1).astype(BF16)
    kvc = compress(xg, w1cat, pos2, cmp_w2.astype(BF16))
    kc = kvc[0]
    vcT = kvc[1].transpose(0, 2, 1)

    qT = q.T
    ocT, selT = cmp_select(qT, kc, vcT, ovT, S)

    kt = min(512, S)
    vselT = v_all[:, kvw:2 * kvw].reshape(S // kt, kt, G, HEAD_DIM).transpose(2, 0, 3, 1)
    vwinT = v_all[:, 2 * kvw:].reshape(S // LANES, LANES, G, HEAD_DIM).transpose(2, 0, 3, 1)
    gates = small[:, :3 * NSA_HEADS].reshape(S, G, NSA_HPG, 3).transpose(3, 1, 2, 0)
    gatesT = jnp.concatenate([gates, jnp.zeros((3, G, SUBLANES - NSA_HPG, S), F32)], axis=2)
    a_nsa = nsa_main(qT, k_all, vselT, selT, vwinT, ocT, gatesT, z_nsa, S, kt=kt)

    n_gate = 3 * NSA_HEADS
    pad_l = n_gate + DN_HEADS
    alog_row = jnp.concatenate([jnp.zeros((pad_l,), F32), dn_a_log.astype(F32),
                                jnp.zeros((LANES - pad_l - DN_HEADS,), F32)]).reshape(1, LANES)
    dtb_row = jnp.concatenate([jnp.zeros((pad_l,), F32), dn_dt_bias.astype(F32),
                               jnp.zeros((LANES - pad_l - DN_HEADS,), F32)]).reshape(1, LANES)
    u, wqd, ak, gl = dn_prep(dnqkv, dn_conv_w, small, alog_row, dtb_row, S)
    a_dn = dn_scan(u, wqd, ak, gl, z_dn, dn_out_norm_g, S)

    mixed = matmul_merge(a_nsa, w_branch_nsa.astype(BF16), a_dn, w_branch_dn.astype(BF16), merge, BF16)
    return matmul_resid(mixed, w_out.astype(BF16), x)


def kernel(x, norm_g, w_in, cmp_pos, cmp_w1, cmp_w2, dn_conv_w, dn_a_log, dn_dt_bias,
           dn_out_norm_g, w_branch_nsa, w_branch_dn, w_out, final_norm_g):
    B, S, D = x.shape
    assert B == 1 and D == D_MODEL
    consts = (_rope_tables(S), _overlap_T(S))
    xs = x[0]
    for l in range(norm_g.shape[0]):
        xs = _layer(xs, norm_g[l], w_in[l], cmp_pos[l], cmp_w1[l], cmp_w2[l], dn_conv_w[l],
                    dn_a_log[l], dn_dt_bias[l], dn_out_norm_g[l], w_branch_nsa[l],
                    w_branch_dn[l], w_out[l], consts)
    return rmsnorm(xs, final_norm_g, F32)[None]
```

```python
import functools
import math

import numpy as np
import jax
import jax.numpy as jnp
from jax import lax
from jax.experimental import pallas as pl
from jax.experimental.pallas import tpu as pltpu

D_MODEL = 4096
HEAD_DIM = 128
NSA_HEADS = 16
NSA_GROUPS = 4
NSA_HPG = 4
NSA_WIDTH = NSA_HEADS * HEAD_DIM
NSA_KV_WIDTH = NSA_GROUPS * HEAD_DIM
CMP_BLOCK = 32
CMP_STRIDE = 16
CMP_HIDDEN = 256
SEL_BLOCK = 64
SEL_TOPK = 16
SEL_LOCAL = 2
SEL_FORCE_SCORE = 1e4
SEL_FORCE_COUNT = 1 + SEL_LOCAL
WINDOW = 512
ROT_DIM = 32
ROPE_THETA = 500000.0
DN_HEADS = 16
DN_WIDTH = DN_HEADS * HEAD_DIM
DN_CHUNK = 64
CONV_WIDTH = 4
NORM_EPS = 1e-6
ATTN_SCALE = HEAD_DIM ** -0.5

LANES = 128
SUBLANES = 8
VMEM_LIMIT_BYTES = 56 * 1024 * 1024

NEG_BIG = -1e30
SEL_KEY_TILE = 512
CMP_ROW_TILE = 512
BF16 = jnp.bfloat16
F32 = jnp.float32


def _cparams(*sem):
    return pltpu.CompilerParams(dimension_semantics=sem, vmem_limit_bytes=VMEM_LIMIT_BYTES)


def _dot(a, b):
    return jnp.dot(a.astype(BF16), b.astype(BF16), preferred_element_type=F32)


def _dot_nt(a, b):
    return lax.dot_general(a.astype(BF16), b.astype(BF16), (((1,), (1,)), ((), ())),
                           preferred_element_type=F32)


def _silu(x):
    return x * jax.nn.sigmoid(x)


def _softplus(x):
    return jnp.maximum(x, 0.0) + jnp.log1p(jnp.exp(-jnp.abs(x)))


def _rmsnorm_kernel(x_ref, g_ref, o_ref):
    x = x_ref[...]
    ms = jnp.mean(x * x, axis=-1, keepdims=True)
    o_ref[...] = (x * lax.rsqrt(ms + NORM_EPS) * g_ref[...]).astype(o_ref.dtype)


def rmsnorm(x, g, out_dtype, rows=256):
    S, D = x.shape
    rows = min(rows, S)
    return pl.pallas_call(
        _rmsnorm_kernel,
        grid=(S // rows,),
        in_specs=[pl.BlockSpec((rows, D), lambda i: (i, 0)),
                  pl.BlockSpec((1, D), lambda i: (0, 0))],
        out_specs=pl.BlockSpec((rows, D), lambda i: (i, 0)),
        out_shape=jax.ShapeDtypeStruct((S, D), out_dtype),
        compiler_params=_cparams("parallel"),
        name="rmsnorm",
    )(x, g.reshape(1, D))


def _mm_kernel(a_ref, b_ref, o_ref):
    o_ref[...] = jnp.dot(a_ref[...], b_ref[...], preferred_element_type=F32).astype(o_ref.dtype)


def _rope_tile(x, c, s1, s2):
    return x * c + pltpu.roll(x, LANES - ROT_DIM // 2, 1) * s1 + pltpu.roll(x, ROT_DIM // 2, 1) * s2


def _mm_rope_kernel(a_ref, b_ref, c_ref, s1_ref, s2_ref, o_ref, *, out_scale):
    acc = jnp.dot(a_ref[...], b_ref[...], preferred_element_type=F32)
    c, s1, s2 = c_ref[...], s1_ref[...], s2_ref[...]
    for h in range(acc.shape[1] // LANES):
        sl = slice(h * LANES, (h + 1) * LANES)
        o_ref[:, sl] = (_rope_tile(acc[:, sl], c, s1, s2) * out_scale).astype(o_ref.dtype)


def _mm_tiles(M, N, tm, tn):
    tm = min(tm, M)
    tn = min(tn, N)
    assert M % tm == 0 and N % tn == 0, (M, N, tm, tn)
    return tm, tn


def matmul(a, b, out_dtype, tm=1024, tn=512, name="matmul"):
    M, K = a.shape
    N = b.shape[1]
    tm, tn = _mm_tiles(M, N, tm, tn)
    return pl.pallas_call(
        _mm_kernel,
        grid=(M // tm, N // tn),
        in_specs=[pl.BlockSpec((tm, K), lambda i, j: (i, 0)),
                  pl.BlockSpec((K, tn), lambda i, j: (0, j))],
        out_specs=pl.BlockSpec((tm, tn), lambda i, j: (i, j)),
        out_shape=jax.ShapeDtypeStruct((M, N), out_dtype),
        compiler_params=_cparams("parallel", "arbitrary"),
        name=name,
    )(a, b)


def matmul_rope(a, b, rope_tabs, out_dtype, out_scale=1.0, tm=1024, tn=512, name="matmul_rope"):
    M, K = a.shape
    N = b.shape[1]
    tm, tn = _mm_tiles(M, N, tm, tn)
    tab_spec = pl.BlockSpec((tm, LANES), lambda i, j: (i, 0))
    return pl.pallas_call(
        functools.partial(_mm_rope_kernel, out_scale=out_scale),
        grid=(M // tm, N // tn),
        in_specs=[pl.BlockSpec((tm, K), lambda i, j: (i, 0)),
                  pl.BlockSpec((K, tn), lambda i, j: (0, j)),
                  tab_spec, tab_spec, tab_spec],
        out_specs=pl.BlockSpec((tm, tn), lambda i, j: (i, j)),
        out_shape=jax.ShapeDtypeStruct((M, N), out_dtype),
        compiler_params=_cparams("parallel", "arbitrary"),
        name=name,
    )(a, b, *rope_tabs)


def _mm_merge_kernel(a1_ref, w1_ref, a2_ref, w2_ref, g1_ref, g2_ref, o_ref):
    y1 = jnp.dot(a1_ref[...], w1_ref[...], preferred_element_type=F32)
    y2 = jnp.dot(a2_ref[...], w2_ref[...], preferred_element_type=F32)
    o_ref[...] = (jax.nn.sigmoid(g1_ref[...]) * y1 + jax.nn.sigmoid(g2_ref[...]) * y2).astype(o_ref.dtype)


def matmul_merge(a1, w1, a2, w2, merge, out_dtype, tm=1024, tn=512):
    M, K = a1.shape
    N = w1.shape[1]
    tm, tn = _mm_tiles(M, N, tm, tn)
    nj = N // tn
    return pl.pallas_call(
        _mm_merge_kernel,
        grid=(M // tm, nj),
        in_specs=[pl.BlockSpec((tm, K), lambda i, j: (i, 0)),
                  pl.BlockSpec((K, tn), lambda i, j: (0, j)),
                  pl.BlockSpec((tm, K), lambda i, j: (i, 0)),
                  pl.BlockSpec((K, tn), lambda i, j: (0, j)),
                  pl.BlockSpec((tm, tn), lambda i, j: (i, j)),
                  pl.BlockSpec((tm, tn), lambda i, j: (i, j + nj))],
        out_specs=pl.BlockSpec((tm, tn), lambda i, j: (i, j)),
        out_shape=jax.ShapeDtypeStruct((M, N), out_dtype),
        compiler_params=_cparams("parallel", "arbitrary"),
        name="matmul_merge",
    )(a1, w1, a2, w2, merge, merge)


def _mm_resid_kernel(a_ref, b_ref, x_ref, o_ref):
    o_ref[...] = x_ref[...] + jnp.dot(a_ref[...], b_ref[...], preferred_element_type=F32)


def matmul_resid(a, b, x, tm=1024, tn=512):
    M, K = a.shape
    N = b.shape[1]
    tm, tn = _mm_tiles(M, N, tm, tn)
    return pl.pallas_call(
        _mm_resid_kernel,
        grid=(M // tm, N // tn),
        in_specs=[pl.BlockSpec((tm, K), lambda i, j: (i, 0)),
                  pl.BlockSpec((K, tn), lambda i, j: (0, j)),
                  pl.BlockSpec((tm, tn), lambda i, j: (i, j))],
        out_specs=pl.BlockSpec((tm, tn), lambda i, j: (i, j)),
        out_shape=jax.ShapeDtypeStruct((M, N), F32),
        compiler_params=_cparams("parallel", "arbitrary"),
        name="matmul_resid",
    )(a, b, x)


def _compress_kernel(x_ref, w1_ref, pos_ref, w2_ref, o_ref):
    ncp = x_ref.shape[2]
    uv = jnp.dot(x_ref[0, 0], w1_ref[0], preferred_element_type=F32)
    bias = jnp.dot(pos_ref[0], w1_ref[0], preferred_element_type=F32)
    top = uv[:, :CMP_HIDDEN] + bias[0:1, :CMP_HIDDEN]
    bot = uv[:, CMP_HIDDEN:]
    bias_bot = bias[1:2, CMP_HIDDEN:]
    bot_next = pltpu.roll(bot, ncp - 1, 0)
    hid = _silu(top + bot_next + bias_bot)
    o_ref[0, 0] = jnp.dot(hid.astype(BF16), w2_ref[0], preferred_element_type=F32).astype(o_ref.dtype)


def compress(xg, w1cat, pos2, w2):
    _, G, ncp, _ = xg.shape
    half = CMP_STRIDE * HEAD_DIM
    return pl.pallas_call(
        _compress_kernel,
        grid=(2, G),
        in_specs=[pl.BlockSpec((1, 1, ncp, half), lambda a, g: (a, g, 0, 0)),
                  pl.BlockSpec((1, half, 2 * CMP_HIDDEN), lambda a, g: (a, 0, 0)),
                  pl.BlockSpec((1, SUBLANES, half), lambda a, g: (a, 0, 0)),
                  pl.BlockSpec((1, CMP_HIDDEN, HEAD_DIM), lambda a, g: (a, 0, 0))],
        out_specs=pl.BlockSpec((1, 1, ncp, HEAD_DIM), lambda a, g: (a, g, 0, 0)),
        out_shape=jax.ShapeDtypeStruct((2, G, ncp, HEAD_DIM), BF16),
        compiler_params=_cparams("parallel", "parallel"),
        name="nsa_compress",
    )(xg, w1cat, pos2, w2)


def _heads_on_lanes(ref, tq):
    return jnp.concatenate([ref[h * HEAD_DIM:(h + 1) * HEAD_DIM, :] for h in range(NSA_HPG)], axis=1)


def _cmp_select_kernel(qT_ref, kc_ref, vcT_ref, ovT_ref, ocT_ref, selT_ref, s_ref, oc_ref, imp_ref,
                       *, tq, rt, n_top):
    qi = pl.program_id(1)
    n_sel = ovT_ref.shape[1]
    q4 = _heads_on_lanes(qT_ref, tq)
    pos = qi * tq + lax.broadcasted_iota(jnp.int32, (1, tq), 1)
    c_max = (qi * tq + tq - CMP_BLOCK) // CMP_STRIDE
    n_rt = (c_max + rt) // rt

    def rows(r):
        return pl.ds(pl.multiple_of(r * rt, rt), rt)

    def scores(r, m):
        s = jnp.dot(kc_ref[0, rows(r), :], q4, preferred_element_type=F32)
        c_idx = r * rt + lax.broadcasted_iota(jnp.int32, (rt, tq), 0)
        bias = jnp.where(c_idx * CMP_STRIDE + (CMP_BLOCK - 1) <= pos, 0.0, NEG_BIG)
        sm = s + jnp.concatenate([bias] * NSA_HPG, axis=1)
        s_ref[rows(r), :] = sm
        return jnp.maximum(m, jnp.max(sm, axis=0, keepdims=True))

    m = lax.fori_loop(0, n_rt, scores, jnp.full((1, NSA_HPG * tq), NEG_BIG, F32))

    def expsum(r, l):
        p = jnp.exp2(s_ref[rows(r), :] - m)
        s_ref[rows(r), :] = p
        return l + jnp.sum(p, axis=0, keepdims=True)

    l = lax.fori_loop(0, n_rt, expsum, jnp.zeros((1, NSA_HPG * tq), F32))
    rl = jnp.where(m > 0.5 * NEG_BIG, 1.0 / l, 0.0)

    oc_ref[...] = jnp.zeros_like(oc_ref)
    imp_ref[...] = jnp.zeros_like(imp_ref)

    def accumulate(r, carry):
        pn = s_ref[rows(r), :] * rl
        oc_ref[...] += jnp.dot(vcT_ref[0, r], pn.astype(BF16), preferred_element_type=F32)
        psum = pn[:, 0:tq]
        for h in range(1, NSA_HPG):
            psum = psum + pn[:, h * tq:(h + 1) * tq]
        p_hi = psum.astype(BF16)
        p_lo = (psum - p_hi.astype(F32)).astype(BF16)
        ovT = ovT_ref[r]
        imp_ref[...] += (jnp.dot(ovT, p_hi, preferred_element_type=F32)
                         + jnp.dot(ovT, p_lo, preferred_element_type=F32))
        return carry

    lax.fori_loop(0, n_rt, accumulate, 0)
    for h in range(NSA_HPG):
        ocT_ref[h * HEAD_DIM:(h + 1) * HEAD_DIM, :] = oc_ref[:, h * tq:(h + 1) * tq]
    imp = imp_ref[...]
    j_idx = lax.broadcasted_iota(jnp.int32, (n_sel, tq), 0)
    cur = pos // SEL_BLOCK
    sel_valid = j_idx * SEL_BLOCK <= pos
    forced = (j_idx == 0) | ((j_idx <= cur) & (j_idx > cur - SEL_LOCAL))
    base = jnp.where(sel_valid & ~forced, imp, -1.0)
    work = base
    jf = j_idx.astype(F32)
    for _ in range(max(n_top - SEL_FORCE_COUNT, 0)):
        mx = jnp.max(work, axis=0, keepdims=True)
        first = jnp.min(jnp.where(work == mx, jf, float(n_sel)), axis=0, keepdims=True)
        work = jnp.where(jf == first, -2.0, work)
    picked = (work != base) & (base >= 0.0)
    selT_ref[0] = jnp.where((forced & sel_valid) | picked, 1.0, 0.0)


def cmp_select(qT, kc, vcT, ovT, S, tq=128):
    G, ncp, _ = kc.shape
    n_rt, n_sel, rt = ovT.shape
    n_top = min(SEL_TOPK, n_sel)
    kern = functools.partial(_cmp_select_kernel, tq=tq, rt=rt, n_top=n_top)
    return pl.pallas_call(
        kern,
        grid=(G, S // tq),
        in_specs=[pl.BlockSpec((NSA_HPG * HEAD_DIM, tq), lambda g, i: (g, i)),
                  pl.BlockSpec((1, ncp, HEAD_DIM), lambda g, i: (g, 0, 0)),
                  pl.BlockSpec((1, n_rt, HEAD_DIM, rt), lambda g, i: (g, 0, 0, 0)),
                  pl.BlockSpec((n_rt, n_sel, rt), lambda g, i: (0, 0, 0))],
        out_specs=[pl.BlockSpec((NSA_HPG * HEAD_DIM, tq), lambda g, i: (g, i)),
                   pl.BlockSpec((1, n_sel, tq), lambda g, i: (g, 0, i))],
        out_shape=[jax.ShapeDtypeStruct((NSA_WIDTH, S), F32),
                   jax.ShapeDtypeStruct((G, n_sel, S), F32)],
        scratch_shapes=[pltpu.VMEM((ncp, NSA_HPG * tq), F32),
                        pltpu.VMEM((HEAD_DIM, NSA_HPG * tq), F32),
                        pltpu.VMEM((n_sel, tq), F32)],
        compiler_params=_cparams("parallel", "arbitrary"),
        name="nsa_cmp_select",
    )(qT, kc, vcT, ovT)


NSA_SUBTILES = 2
NSA_SCRATCH_PER_SUBTILE = 8


class _SelSweep:
    def __init__(self, q4, tpos, lanes, kaug_ref, vselT_ref, selT_ref, scratch, tq, kt):
        self.q4, self.tpos, self.lanes, self.tq, self.kt = q4, tpos, lanes, tq, kt
        self.kaug_ref, self.vselT_ref, self.selT_ref = kaug_ref, vselT_ref, selT_ref
        (self.qaug_ref, s0, s1, p0, p1, self.acc_ref, self.m_ref, self.l_ref) = scratch
        self.s_bufs, self.p_bufs = (s0, s1), (p0, p1)

    def scores(self, i, s_dst):
        bpt = self.kt // SEL_BLOCK
        sel8 = self.selT_ref[0, pl.ds(pl.multiple_of(i * bpt, bpt), bpt), self.lanes]
        neg = jnp.where(sel8 > 0.5, 0.0, NEG_BIG)
        neg = jnp.concatenate([neg, jnp.zeros((MASK_ROWS - bpt, self.tq), F32)], axis=0).astype(BF16)
        self.qaug_ref[HEAD_DIM:HEAD_DIM + MASK_ROWS, :] = jnp.concatenate([neg] * NSA_HPG, axis=1)
        k_t = self.kaug_ref[0, pl.ds(pl.multiple_of(i * self.kt, self.kt), self.kt), :]
        s_dst[...] = jnp.dot(k_t, self.qaug_ref[...], preferred_element_type=F32)

    def values(self, i, p_src):
        return jnp.dot(self.vselT_ref[0, i], p_src[...], preferred_element_type=F32)

    def start(self):
        self.qaug_ref[0:HEAD_DIM, :] = self.q4
        self.qaug_ref[HEAD_DIM:, :] = jnp.zeros((AUG_DEPTH - HEAD_DIM, NSA_HPG * self.tq), BF16)
        self.acc_ref[...] = jnp.zeros_like(self.acc_ref)
        self.m_ref[...] = jnp.full_like(self.m_ref, NEG_BIG)
        self.l_ref[...] = jnp.zeros_like(self.l_ref)
        self.p_bufs[1][...] = jnp.zeros_like(self.p_bufs[1])
        self.scores(0, self.s_bufs[0])

    def step(self, t, cur, diagonal):
        nxt = 1 - cur
        if not diagonal:
            self.scores(t + 1, self.s_bufs[nxt])
        pv_prev = self.values(jnp.maximum(t - 1, 0), self.p_bufs[nxt])
        sm = self.s_bufs[cur][...]
        if diagonal:
            kpos = t * self.kt + lax.broadcasted_iota(jnp.int32, (self.kt, self.tq), 0)
            causal = jnp.where(kpos <= self.tpos, 0.0, NEG_BIG)
            sm = sm + jnp.concatenate([causal] * NSA_HPG, axis=1)
        m_old = self.m_ref[...]
        m_new = jnp.maximum(m_old, jnp.max(sm, axis=0, keepdims=True))
        alpha = jnp.exp2(m_old - m_new)
        p = jnp.exp2(sm - m_new)
        self.p_bufs[cur][...] = p.astype(BF16)
        self.l_ref[...] = self.l_ref[...] * alpha + jnp.sum(p, axis=0, keepdims=True)
        self.acc_ref[...] = (self.acc_ref[...] + pv_prev) * alpha
        self.m_ref[...] = m_new

    def last_step(self, t, cur):
        self.step(t, cur, True)
        self.acc_ref[...] += self.values(t, self.p_bufs[cur])

    def result(self):
        return self.acc_ref[...] / self.l_ref[...]


def _nsa_main_kernel(qT_ref, kaug_ref, vselT_ref, selT_ref, kwin_ref, vwinT_ref, ocT_ref,
                     gate_ref, z_ref, o_ref, *scratch, tq, kt):
    qi = pl.program_id(1)
    base = qi * (NSA_SUBTILES * tq)
    subs = []
    for a in range(NSA_SUBTILES):
        lanes = slice(a * tq, (a + 1) * tq)
        q4 = jnp.concatenate([qT_ref[h * HEAD_DIM:(h + 1) * HEAD_DIM, lanes] for h in range(NSA_HPG)], axis=1)
        tpos = base + a * tq + lax.broadcasted_iota(jnp.int32, (1, tq), 1)
        sc = scratch[a * NSA_SCRATCH_PER_SUBTILE:(a + 1) * NSA_SCRATCH_PER_SUBTILE]
        subs.append(_SelSweep(q4, tpos, lanes, kaug_ref, vselT_ref, selT_ref, sc, tq, kt))

    last = base // kt
    for sw in subs:
        sw.start()

    def pair(j, carry):
        for cur in range(2):
            for sw in subs:
                sw.step(2 * j + cur, cur, False)
        return carry

    lax.fori_loop(0, last // 2, pair, 0)

    @pl.when(last % 2 == 1)
    def _():
        for sw in subs:
            sw.step(last - 1, 0, False)
        for sw in subs:
            sw.last_step(last, 1)

    @pl.when(last % 2 == 0)
    def _():
        for sw in subs:
            sw.last_step(last, 0)

    for a, sel in enumerate(subs):
        q4, tpos, lanes = sel.q4, sel.tpos, sel.lanes
        o_sel = sel.result()
        s0 = base + a * tq

        n_wt = (WINDOW + tq) // LANES
        t0 = jnp.maximum(s0 // LANES - WINDOW // LANES, 0)
        kw = kwin_ref[pl.ds(pl.multiple_of(t0 * LANES, LANES), WINDOW + tq), :]
        sw = jnp.dot(kw, q4, preferred_element_type=F32)
        wpos = t0 * LANES + lax.broadcasted_iota(jnp.int32, (WINDOW + tq, tq), 0)
        okw = (wpos <= tpos) & (wpos > tpos - WINDOW)
        biasw = jnp.where(okw, 0.0, NEG_BIG)
        sw = sw + jnp.concatenate([biasw] * NSA_HPG, axis=1)
        mw = jnp.max(sw, axis=0, keepdims=True)
        pw = jnp.exp2(sw - mw)
        lw = jnp.sum(pw, axis=0, keepdims=True)
        vwT = jnp.concatenate([vwinT_ref[0, t0 + j] for j in range(n_wt)], axis=1)
        o_win = jnp.dot(vwT, pw.astype(BF16), preferred_element_type=F32) / lw

        rows = slice(a * tq, (a + 1) * tq)
        for h in range(NSA_HPG):
            hs = slice(h * tq, (h + 1) * tq)
            g0 = jax.nn.sigmoid(gate_ref[0, 0, h:h + 1, lanes])
            g1 = jax.nn.sigmoid(gate_ref[1, 0, h:h + 1, lanes])
            g2 = jax.nn.sigmoid(gate_ref[2, 0, h:h + 1, lanes])
            oT = (ocT_ref[h * HEAD_DIM:(h + 1) * HEAD_DIM, lanes] * g0 + o_sel[:, hs] * g1
                  + o_win[:, hs] * g2)
            cs = slice(h * HEAD_DIM, (h + 1) * HEAD_DIM)
            o_ref[rows, cs] = (oT.T * _silu(z_ref[rows, cs])).astype(o_ref.dtype)


MASK_ROWS = 16
AUG_DEPTH = 2 * HEAD_DIM


def nsa_main(qT, kaug, k_all, vselT, selT, vwinT, ocT, gatesT, z, S, tq=128, kt=512):
    G = NSA_GROUPS
    kt = min(kt, S)
    n_sel = S // SEL_BLOCK
    tqs = NSA_SUBTILES * tq
    assert tq == LANES and S % kt == 0 and S >= WINDOW + tq and kt // SEL_BLOCK <= MASK_ROWS
    assert kt % tqs == 0 and S % tqs == 0
    kern = functools.partial(_nsa_main_kernel, tq=tq, kt=kt)
    gw = NSA_HPG * HEAD_DIM
    ql = NSA_HPG * tq
    per_subtile = [pltpu.VMEM((AUG_DEPTH, ql), BF16),
                   pltpu.VMEM((kt, ql), F32), pltpu.VMEM((kt, ql), F32),
                   pltpu.VMEM((kt, ql), BF16), pltpu.VMEM((kt, ql), BF16),
                   pltpu.VMEM((HEAD_DIM, ql), F32),
                   pltpu.VMEM((1, ql), F32), pltpu.VMEM((1, ql), F32)]
    assert len(per_subtile) == NSA_SCRATCH_PER_SUBTILE
    return pl.pallas_call(
        kern,
        grid=(G, S // tqs),
        in_specs=[pl.BlockSpec((gw, tqs), lambda g, i: (g, i)),
                  pl.BlockSpec((1, S, AUG_DEPTH), lambda g, i: (g, 0, 0)),
                  pl.BlockSpec((1, S // kt, HEAD_DIM, kt), lambda g, i: (g, 0, 0, 0)),
                  pl.BlockSpec((1, n_sel, tqs), lambda g, i: (g, 0, i)),
                  pl.BlockSpec((S, HEAD_DIM), lambda g, i: (0, 2 * G + g)),
                  pl.BlockSpec((1, S // LANES, HEAD_DIM, LANES), lambda g, i: (g, 0, 0, 0)),
                  pl.BlockSpec((gw, tqs), lambda g, i: (g, i)),
                  pl.BlockSpec((3, 1, SUBLANES, tqs), lambda g, i: (0, g, 0, i)),
                  pl.BlockSpec((tqs, gw), lambda g, i: (i, g))],
        out_specs=pl.BlockSpec((tqs, gw), lambda g, i: (i, g)),
        out_shape=jax.ShapeDtypeStruct((S, NSA_WIDTH), BF16),
        scratch_shapes=per_subtile * NSA_SUBTILES,
        compiler_params=_cparams("parallel", "arbitrary"),
        name="nsa_main",
    )(qT, kaug, vselT, selT, k_all, vwinT, ocT, gatesT, z)


DN_PAIR = 2 * DN_CHUNK


def _lane_pick(x, lane):
    li = lax.broadcasted_iota(jnp.int32, x.shape, 1)
    col = jnp.sum(jnp.where(li == lane, x, 0.0), axis=1, keepdims=True)
    return jnp.broadcast_to(col, x.shape)


def _conv_silu(cur, halo, w):
    r8 = lax.broadcasted_iota(jnp.int32, (SUBLANES, LANES), 0)
    y = cur * w[CONV_WIDTH - 1:CONV_WIDTH]
    for s in range(1, CONV_WIDTH):
        xs = pltpu.roll(cur, s, 0)
        top = jnp.where(r8 < s, pltpu.roll(halo, s, 0), xs[0:SUBLANES])
        xs = jnp.concatenate([top, xs[SUBLANES:]], axis=0)
        y = y + xs * w[CONV_WIDTH - 1 - s:CONV_WIDTH - s]
    return _silu(y)


def _dn_prep_kernel(q_ref, qh_ref, k_ref, kh_ref, v_ref, vh_ref, wq_ref, wk_ref, wv_ref,
                    small_ref, alog_ref, dtb_ref,
                    u_ref, wqd_ref, ak_ref, gl_ref, *, hb):
    p = pl.program_id(0)
    hblk = pl.program_id(1)
    first = p == 0
    R = DN_PAIR
    ri = lax.broadcasted_iota(jnp.int32, (R, R), 0)
    ci = lax.broadcasted_iota(jnp.int32, (R, R), 1)
    same = (ri // DN_CHUNK) == (ci // DN_CHUNK)
    causal = same & (ci <= ri)
    strict = same & (ci < ri)
    eye = (ri == ci).astype(F32)
    rin = ri % DN_CHUNK
    first_half = ci < DN_CHUNK
    r8 = lax.broadcasted_iota(jnp.int32, (SUBLANES, LANES), 0)

    small = small_ref[...]
    beta_all = jax.nn.sigmoid(small)
    g_all = -jnp.exp(alog_ref[...]) * _softplus(small + dtb_ref[...])

    def halo_of(ref, hs):
        h = ref[:, hs]
        return jnp.where(first, jnp.zeros_like(h), h)

    st = []
    for hh in range(hb):
        hs = slice(hh * LANES, (hh + 1) * LANES)
        head = hblk * hb + hh
        q = _conv_silu(q_ref[:, hs], halo_of(qh_ref, hs), wq_ref[:, hs])
        k = _conv_silu(k_ref[:, hs], halo_of(kh_ref, hs), wk_ref[:, hs])
        v = _conv_silu(v_ref[:, hs], halo_of(vh_ref, hs), wv_ref[:, hs])
        q = q * lax.rsqrt(jnp.sum(q * q, axis=1, keepdims=True) + NORM_EPS) * (HEAD_DIM ** -0.5)
        k = k * lax.rsqrt(jnp.sum(k * k, axis=1, keepdims=True) + NORM_EPS)
        beta = _lane_pick(beta_all, 3 * NSA_HEADS + head)
        gc = _lane_pick(g_all, 3 * NSA_HEADS + DN_HEADS + head)
        sh = 1
        while sh < DN_CHUNK:
            gc = gc + jnp.where(rin >= sh, pltpu.roll(gc, sh, 0), 0.0)
            sh *= 2
        g_last = jnp.concatenate(
            [jnp.broadcast_to(gc[DN_CHUNK - 1:DN_CHUNK, :], (DN_CHUNK, LANES)),
             jnp.broadcast_to(gc[R - 1:R, :], (DN_CHUNK, LANES))], axis=0)
        e_g = jnp.exp(gc)
        decay = jnp.exp(jnp.where(causal, gc - gc.T, -jnp.inf))
        kb = k * beta
        pq = _dot_nt(jnp.concatenate([kb, q], axis=0), k)
        a = jnp.where(strict, pq[:R] * decay, 0.0)
        attn = jnp.where(causal, pq[R:] * decay, 0.0)
        qd = q * e_g
        kdT = (k * jnp.exp(g_last - gc)).T
        wqd_half = (qd[:DN_CHUNK], qd[DN_CHUNK:])
        ak_ref[:, hs] = jnp.concatenate(
            [attn[:DN_CHUNK], jnp.where(first_half, kdT, 0.0),
             attn[DN_CHUNK:], jnp.where(first_half, 0.0, kdT)], axis=0).astype(BF16)
        gl_ref[0, hh] = jnp.where(r8 == 0, jnp.exp(g_last[0:SUBLANES]),
                                  jnp.where(r8 == 1, jnp.exp(g_last[DN_CHUNK:DN_CHUNK + SUBLANES]), 0.0))
        rhs = jnp.concatenate([v * beta, kb * e_g], axis=1).astype(BF16)
        st.append(dict(hs=hs, a=a, rhs=rhs, qd=wqd_half))

    for d in st:
        d["x"] = eye - d["a"]
        d["ap"] = _dot(d["a"], d["a"])
    n_sq = int(math.log2(DN_CHUNK)) - 1
    for s in range(n_sq):
        for d in st:
            if s + 1 < n_sq:
                xa = _dot(jnp.concatenate([d["x"], d["ap"]], axis=0), d["ap"])
                d["x"] = d["x"] + xa[:R]
                d["ap"] = xa[R:]
            else:
                d["x"] = d["x"] + _dot(d["x"], d["ap"])
    for d in st:
        uw = jnp.dot(d["x"].astype(BF16), d["rhs"], preferred_element_type=F32)
        u, w = uw[:, :LANES], uw[:, LANES:]
        qd0, qd1 = d["qd"]
        u_ref[:, d["hs"]] = u
        wqd_ref[:, d["hs"]] = jnp.concatenate(
            [w[:DN_CHUNK], qd0, w[DN_CHUNK:], qd1], axis=0).astype(BF16)


def dn_prep(dnqkv, conv_w, small, alog_row, dtb_row, S, hb=8):
    R = DN_PAIR
    NP = S // R
    H = DN_HEADS
    nhb = H // hb
    bw = hb * LANES
    rpb = R // SUBLANES

    def cur(off):
        return pl.BlockSpec((R, bw), lambda p, h: (p, off * nhb + h))

    def halo(off):
        return pl.BlockSpec((SUBLANES, bw), lambda p, h: (jnp.maximum(p * rpb - 1, 0), off * nhb + h))

    def wspec(off):
        return pl.BlockSpec((CONV_WIDTH, bw), lambda p, h: (0, off * nhb + h))

    row = pl.BlockSpec((1, LANES), lambda p, h: (0, 0))
    kern = functools.partial(_dn_prep_kernel, hb=hb)
    return pl.pallas_call(
        kern,
        grid=(NP, nhb),
        in_specs=[cur(0), halo(0), cur(1), halo(1), cur(2), halo(2), wspec(0), wspec(1), wspec(2),
                  pl.BlockSpec((R, LANES), lambda p, h: (p, 0)), row, row],
        out_specs=[pl.BlockSpec((R, bw), lambda p, h: (p, h)),
                   pl.BlockSpec((2 * R, bw), lambda p, h: (p, h)),
                   pl.BlockSpec((3 * R, bw), lambda p, h: (p, h)),
                   pl.BlockSpec((1, hb, SUBLANES, LANES), lambda p, h: (p, h, 0, 0))],
        out_shape=[jax.ShapeDtypeStruct((S, DN_WIDTH), F32),
                   jax.ShapeDtypeStruct((2 * S, DN_WIDTH), BF16),
                   jax.ShapeDtypeStruct((3 * S, DN_WIDTH), BF16),
                   jax.ShapeDtypeStruct((NP, H, SUBLANES, LANES), F32)],
        compiler_params=_cparams("parallel", "parallel"),
        name="dn_prep",
    )(dnqkv, dnqkv, dnqkv, dnqkv, dnqkv, dnqkv, conv_w, conv_w, conv_w, small, alog_row, dtb_row)


def _dn_scan_kernel(u_ref, wqd_ref, ak_ref, gl_ref, z_ref, ng_ref, o_ref, state_ref):
    p = pl.program_id(0)

    @pl.when(p == 0)
    def _():
        state_ref[...] = jnp.zeros_like(state_ref)

    C = DN_CHUNK
    zeros = jnp.zeros((C, LANES), F32)
    ng = ng_ref[...]
    for h in range(DN_HEADS):
        hs = slice(h * LANES, (h + 1) * LANES)
        st = state_ref[h]
        outs = []
        for c in range(2):
            wqd = wqd_ref[2 * c * C:2 * (c + 1) * C, hs]
            r1 = jnp.dot(wqd, st.astype(BF16), preferred_element_type=F32)
            vn = u_ref[c * C:(c + 1) * C, hs] - r1[:C]
            vn_full = jnp.concatenate([vn, zeros] if c == 0 else [zeros, vn], axis=0)
            ak = ak_ref[3 * c * C:3 * (c + 1) * C, hs]
            r2 = jnp.dot(ak, vn_full.astype(BF16), preferred_element_type=F32)
            outs.append(r1[C:] + r2[:C])
            st = st * gl_ref[0, h, c:c + 1, :] + r2[C:]
        state_ref[h] = st
        o = jnp.concatenate(outs, axis=0)
        o = o * lax.rsqrt(jnp.mean(o * o, axis=1, keepdims=True) + NORM_EPS) * ng
        o_ref[:, hs] = (o * _silu(z_ref[:, hs])).astype(o_ref.dtype)


def dn_scan(u, wqd, ak, gl, z_dn, norm_g, S):
    R = DN_PAIR
    NP = S // R
    W = DN_WIDTH
    return pl.pallas_call(
        _dn_scan_kernel,
        grid=(NP,),
        in_specs=[pl.BlockSpec((R, W), lambda p: (p, 0)),
                  pl.BlockSpec((2 * R, W), lambda p: (p, 0)),
                  pl.BlockSpec((3 * R, W), lambda p: (p, 0)),
                  pl.BlockSpec((1, DN_HEADS, SUBLANES, LANES), lambda p: (p, 0, 0, 0)),
                  pl.BlockSpec((R, W), lambda p: (p, 0)),
                  pl.BlockSpec((1, LANES), lambda p: (0, 0))],
        out_specs=pl.BlockSpec((R, W), lambda p: (p, 0)),
        out_shape=jax.ShapeDtypeStruct((S, W), BF16),
        scratch_shapes=[pltpu.VMEM((DN_HEADS, HEAD_DIM, HEAD_DIM), F32)],
        compiler_params=_cparams("arbitrary"),
        name="dn_scan",
    )(u, wqd, ak, gl, z_dn, norm_g.reshape(1, LANES))


def _rope_tables(S):
    inv_freq = ROPE_THETA ** (-jnp.arange(0, ROT_DIM, 2, dtype=F32) / ROT_DIM)
    ang = jnp.arange(S).astype(F32)[:, None] * inv_freq[None, :]
    cos, sin = jnp.cos(ang), jnp.sin(ang)
    half = ROT_DIM // 2
    rest = LANES - ROT_DIM
    c = jnp.concatenate([cos, cos, jnp.ones((S, rest), F32)], axis=1)
    s1 = jnp.concatenate([-sin, jnp.zeros((S, LANES - half), F32)], axis=1)
    s2 = jnp.concatenate([jnp.zeros((S, half), F32), sin, jnp.zeros((S, rest), F32)], axis=1)
    return c, s1, s2


def _overlap_T(S):
    n_cmp = (S - CMP_BLOCK) // CMP_STRIDE + 1
    ncp = S // CMP_STRIDE
    n_sel = S // SEL_BLOCK
    c = np.arange(ncp)[None, :]
    j = np.arange(n_sel)[:, None]
    ov = (c * CMP_STRIDE < j * SEL_BLOCK + SEL_BLOCK) & (c * CMP_STRIDE + CMP_BLOCK - 1 >= j * SEL_BLOCK) & (c < n_cmp)
    rt = min(CMP_ROW_TILE, ncp)
    ov = ov.reshape(n_sel, ncp // rt, rt).transpose(1, 0, 2)
    return jnp.asarray(ov, dtype=BF16)


def _block_onehot(S, kt):
    s = np.arange(S)[:, None]
    r = np.arange(LANES)[None, :]
    return jnp.asarray((s % kt) // SEL_BLOCK == r, dtype=BF16)


def _consts(S):
    return _rope_tables(S), _overlap_T(S), _block_onehot(S, min(SEL_KEY_TILE, S))


_SEG = np.cumsum([0, NSA_WIDTH, 6 * NSA_KV_WIDTH, 3 * NSA_HEADS, NSA_WIDTH, 3 * DN_WIDTH,
                  DN_HEADS, DN_HEADS, DN_WIDTH, 2 * D_MODEL])


def _split_w_in(w):
    seg = lambda i: w[:, _SEG[i]:_SEG[i + 1]]
    kv = seg(1).reshape(D_MODEL, 6, NSA_KV_WIDTH)
    wk = kv[:, 0::2].reshape(D_MODEL, 3 * NSA_KV_WIDTH)
    wv = kv[:, 1::2].reshape(D_MODEL, 3 * NSA_KV_WIDTH)
    n_small = 3 * NSA_HEADS + 2 * DN_HEADS
    small = jnp.concatenate([seg(2), seg(5), seg(6), jnp.zeros((D_MODEL, LANES - n_small), w.dtype)], axis=1)
    parts = dict(q=seg(0), k=wk, v=wv, small=small, z_nsa=seg(3), dn_qkv=seg(4), z_dn=seg(7), merge=seg(8))
    return {n: p.astype(BF16) for n, p in parts.items()}


def _layer(x, norm_g, w_in, cmp_pos, cmp_w1, cmp_w2, dn_conv_w, dn_a_log, dn_dt_bias,
           dn_out_norm_g, w_branch_nsa, w_branch_dn, w_out, consts):
    S = x.shape[0]
    G = NSA_GROUPS
    rope_tabs, ovT, onehot = consts
    wp = _split_w_in(w_in)
    h = rmsnorm(x, norm_g, BF16)

    q = matmul_rope(h, wp["q"], rope_tabs, BF16, out_scale=ATTN_SCALE * math.log2(math.e),
                    name="proj_q")
    k_all = matmul_rope(h, wp["k"], rope_tabs, BF16, name="proj_k")
    v_all = matmul(h, wp["v"], BF16, name="proj_v")
    small = matmul(h, wp["small"], F32, tn=LANES, name="proj_small")
    z_nsa = matmul(h, wp["z_nsa"], F32, name="proj_z_nsa")
    dnqkv = matmul(h, wp["dn_qkv"], F32, name="proj_dn_qkv")
    z_dn = matmul(h, wp["z_dn"], F32, name="proj_z_dn")
    merge = matmul(h, wp["merge"], F32, name="proj_merge")

    ncp = S // CMP_STRIDE
    kvw = NSA_KV_WIDTH

    def chunked(t):
        return t.reshape(ncp, CMP_STRIDE, G, HEAD_DIM).transpose(2, 0, 1, 3).reshape(G, ncp, CMP_STRIDE * HEAD_DIM)

    xg = jnp.stack([chunked(k_all[:, :kvw]), chunked(v_all[:, :kvw])])
    half = CMP_STRIDE * HEAD_DIM
    w1cat = jnp.concatenate([cmp_w1[:, :half], cmp_w1[:, half:]], axis=2).astype(BF16)
    pos_flat = cmp_pos.reshape(2, 2, half)
    pos2 = jnp.concatenate([pos_flat, jnp.zeros((2, SUBLANES - 2, half), F32)], axis=1).astype(BF16)
    kvc = compress(xg, w1cat, pos2, cmp_w2.astype(BF16))
    kc = kvc[0]
    rt = ovT.shape[2]
    vcT = kvc[1].reshape(G, ncp // rt, rt, HEAD_DIM).transpose(0, 1, 3, 2)

    qT = q.T
    ocT, selT = cmp_select(qT, kc, vcT, ovT, S)

    kt = min(SEL_KEY_TILE, S)
    k_sel = k_all[:, kvw:2 * kvw].reshape(S, G, HEAD_DIM).transpose(1, 0, 2)
    kaug = jnp.concatenate([k_sel, jnp.broadcast_to(onehot[None], (G, S, LANES))], axis=2)
    vselT = v_all[:, kvw:2 * kvw].reshape(S // kt, kt, G, HEAD_DIM).transpose(2, 0, 3, 1)
    vwinT = v_all[:, 2 * kvw:].reshape(S // LANES, LANES, G, HEAD_DIM).transpose(2, 0, 3, 1)
    gates = small[:, :3 * NSA_HEADS].reshape(S, G, NSA_HPG, 3).transpose(3, 1, 2, 0)
    gatesT = jnp.concatenate([gates, jnp.zeros((3, G, SUBLANES - NSA_HPG, S), F32)], axis=2)
    a_nsa = nsa_main(qT, kaug, k_all, vselT, selT, vwinT, ocT, gatesT, z_nsa, S, kt=kt)

    n_gate = 3 * NSA_HEADS
    pad_l = n_gate + DN_HEADS
    alog_row = jnp.concatenate([jnp.zeros((pad_l,), F32), dn_a_log.astype(F32),
                                jnp.zeros((LANES - pad_l - DN_HEADS,), F32)]).reshape(1, LANES)
    dtb_row = jnp.concatenate([jnp.zeros((pad_l,), F32), dn_dt_bias.astype(F32),
                               jnp.zeros((LANES - pad_l - DN_HEADS,), F32)]).reshape(1, LANES)
    u, wqd, ak, gl = dn_prep(dnqkv, dn_conv_w, small, alog_row, dtb_row, S)
    a_dn = dn_scan(u, wqd, ak, gl, z_dn, dn_out_norm_g, S)

    mixed = matmul_merge(a_nsa, w_branch_nsa.astype(BF16), a_dn, w_branch_dn.astype(BF16), merge, BF16)
    return matmul_resid(mixed, w_out.astype(BF16), x)


def kernel(x, norm_g, w_in, cmp_pos, cmp_w1, cmp_w2, dn_conv_w, dn_a_log, dn_dt_bias,
           dn_out_norm_g, w_branch_nsa, w_branch_dn, w_out, final_norm_g):
    B, S, D = x.shape
    assert B == 1 and D == D_MODEL
    consts = _consts(S)
    xs = x[0]
    for l in range(norm_g.shape[0]):
        xs = _layer(xs, norm_g[l], w_in[l], cmp_pos[l], cmp_w1[l], cmp_w2[l], dn_conv_w[l],
                    dn_a_log[l], dn_dt_bias[l], dn_out_norm_g[l], w_branch_nsa[l],
                    w_branch_dn[l], w_out[l], consts)
    return rmsnorm(xs, final_norm_g, F32)[None]
```

```python
import functools
import math

import numpy as np
import jax
import jax.numpy as jnp
from jax import lax
from jax.experimental import pallas as pl
from jax.experimental.pallas import tpu as pltpu

D_MODEL = 4096
HEAD_DIM = 128
NSA_HEADS = 16
NSA_GROUPS = 4
NSA_HPG = 4
NSA_WIDTH = NSA_HEADS * HEAD_DIM
NSA_KV_WIDTH = NSA_GROUPS * HEAD_DIM
CMP_BLOCK = 32
CMP_STRIDE = 16
CMP_HIDDEN = 256
SEL_BLOCK = 64
SEL_TOPK = 16
SEL_LOCAL = 2
SEL_FORCE_SCORE = 1e4
SEL_FORCE_COUNT = 1 + SEL_LOCAL
WINDOW = 512
ROT_DIM = 32
ROPE_THETA = 500000.0
DN_HEADS = 16
DN_WIDTH = DN_HEADS * HEAD_DIM
DN_CHUNK = 64
CONV_WIDTH = 4
NORM_EPS = 1e-6
ATTN_SCALE = HEAD_DIM ** -0.5

LANES = 128
SUBLANES = 8
VMEM_LIMIT_BYTES = 56 * 1024 * 1024

NEG_BIG = -1e30
SEL_KEY_TILE = 512
CMP_ROW_TILE = 512
BF16 = jnp.bfloat16
F32 = jnp.float32


def _cparams(*sem):
    return pltpu.CompilerParams(dimension_semantics=sem, vmem_limit_bytes=VMEM_LIMIT_BYTES)


def _dot(a, b):
    return jnp.dot(a.astype(BF16), b.astype(BF16), preferred_element_type=F32)


def _dot_nt(a, b):
    return lax.dot_general(a.astype(BF16), b.astype(BF16), (((1,), (1,)), ((), ())),
                           preferred_element_type=F32)


def _silu(x):
    return x * jax.nn.sigmoid(x)


def _softplus(x):
    return jnp.maximum(x, 0.0) + jnp.log1p(jnp.exp(-jnp.abs(x)))


def _rmsnorm_kernel(x_ref, g_ref, o_ref):
    x = x_ref[...]
    ms = jnp.mean(x * x, axis=-1, keepdims=True)
    o_ref[...] = (x * lax.rsqrt(ms + NORM_EPS) * g_ref[...]).astype(o_ref.dtype)


def rmsnorm(x, g, out_dtype, rows=256):
    S, D = x.shape
    rows = min(rows, S)
    return pl.pallas_call(
        _rmsnorm_kernel,
        grid=(S // rows,),
        in_specs=[pl.BlockSpec((rows, D), lambda i: (i, 0)),
                  pl.BlockSpec((1, D), lambda i: (0, 0))],
        out_specs=pl.BlockSpec((rows, D), lambda i: (i, 0)),
        out_shape=jax.ShapeDtypeStruct((S, D), out_dtype),
        compiler_params=_cparams("parallel"),
        name="rmsnorm",
    )(x, g.reshape(1, D))


def _mm_kernel(a_ref, b_ref, o_ref):
    o_ref[...] = jnp.dot(a_ref[...], b_ref[...], preferred_element_type=F32).astype(o_ref.dtype)


def _rope_tile(x, c, s1, s2):
    return x * c + pltpu.roll(x, LANES - ROT_DIM // 2, 1) * s1 + pltpu.roll(x, ROT_DIM // 2, 1) * s2


def _mm_rope_kernel(a_ref, b_ref, c_ref, s1_ref, s2_ref, o_ref, *, out_scale):
    acc = jnp.dot(a_ref[...], b_ref[...], preferred_element_type=F32)
    c, s1, s2 = c_ref[...], s1_ref[...], s2_ref[...]
    for h in range(acc.shape[1] // LANES):
        sl = slice(h * LANES, (h + 1) * LANES)
        o_ref[:, sl] = (_rope_tile(acc[:, sl], c, s1, s2) * out_scale).astype(o_ref.dtype)


def _mm_tiles(M, N, tm, tn):
    tm = min(tm, M)
    tn = min(tn, N)
    assert M % tm == 0 and N % tn == 0, (M, N, tm, tn)
    return tm, tn


def matmul(a, b, out_dtype, tm=1024, tn=512, name="matmul"):
    M, K = a.shape
    N = b.shape[1]
    tm, tn = _mm_tiles(M, N, tm, tn)
    return pl.pallas_call(
        _mm_kernel,
        grid=(M // tm, N // tn),
        in_specs=[pl.BlockSpec((tm, K), lambda i, j: (i, 0)),
                  pl.BlockSpec((K, tn), lambda i, j: (0, j))],
        out_specs=pl.BlockSpec((tm, tn), lambda i, j: (i, j)),
        out_shape=jax.ShapeDtypeStruct((M, N), out_dtype),
        compiler_params=_cparams("parallel", "arbitrary"),
        name=name,
    )(a, b)


def matmul_rope(a, b, rope_tabs, out_dtype, out_scale=1.0, tm=1024, tn=512, name="matmul_rope"):
    M, K = a.shape
    N = b.shape[1]
    tm, tn = _mm_tiles(M, N, tm, tn)
    tab_spec = pl.BlockSpec((tm, LANES), lambda i, j: (i, 0))
    return pl.pallas_call(
        functools.partial(_mm_rope_kernel, out_scale=out_scale),
        grid=(M // tm, N // tn),
        in_specs=[pl.BlockSpec((tm, K), lambda i, j: (i, 0)),
                  pl.BlockSpec((K, tn), lambda i, j: (0, j)),
                  tab_spec, tab_spec, tab_spec],
        out_specs=pl.BlockSpec((tm, tn), lambda i, j: (i, j)),
        out_shape=jax.ShapeDtypeStruct((M, N), out_dtype),
        compiler_params=_cparams("parallel", "arbitrary"),
        name=name,
    )(a, b, *rope_tabs)


def _mm_merge_kernel(a1_ref, w1_ref, a2_ref, w2_ref, g1_ref, g2_ref, o_ref):
    y1 = jnp.dot(a1_ref[...], w1_ref[...], preferred_element_type=F32)
    y2 = jnp.dot(a2_ref[...], w2_ref[...], preferred_element_type=F32)
    o_ref[...] = (jax.nn.sigmoid(g1_ref[...]) * y1 + jax.nn.sigmoid(g2_ref[...]) * y2).astype(o_ref.dtype)


def matmul_merge(a1, w1, a2, w2, merge, out_dtype, tm=1024, tn=512):
    M, K = a1.shape
    N = w1.shape[1]
    tm, tn = _mm_tiles(M, N, tm, tn)
    nj = N // tn
    return pl.pallas_call(
        _mm_merge_kernel,
        grid=(M // tm, nj),
        in_specs=[pl.BlockSpec((tm, K), lambda i, j: (i, 0)),
                  pl.BlockSpec((K, tn), lambda i, j: (0, j)),
                  pl.BlockSpec((tm, K), lambda i, j: (i, 0)),
                  pl.BlockSpec((K, tn), lambda i, j: (0, j)),
                  pl.BlockSpec((tm, tn), lambda i, j: (i, j)),
                  pl.BlockSpec((tm, tn), lambda i, j: (i, j + nj))],
        out_specs=pl.BlockSpec((tm, tn), lambda i, j: (i, j)),
        out_shape=jax.ShapeDtypeStruct((M, N), out_dtype),
        compiler_params=_cparams("parallel", "arbitrary"),
        name="matmul_merge",
    )(a1, w1, a2, w2, merge, merge)


def _mm_resid_kernel(a_ref, b_ref, x_ref, o_ref):
    o_ref[...] = x_ref[...] + jnp.dot(a_ref[...], b_ref[...], preferred_element_type=F32)


def matmul_resid(a, b, x, tm=1024, tn=512):
    M, K = a.shape
    N = b.shape[1]
    tm, tn = _mm_tiles(M, N, tm, tn)
    return pl.pallas_call(
        _mm_resid_kernel,
        grid=(M // tm, N // tn),
        in_specs=[pl.BlockSpec((tm, K), lambda i, j: (i, 0)),
                  pl.BlockSpec((K, tn), lambda i, j: (0, j)),
                  pl.BlockSpec((tm, tn), lambda i, j: (i, j))],
        out_specs=pl.BlockSpec((tm, tn), lambda i, j: (i, j)),
        out_shape=jax.ShapeDtypeStruct((M, N), F32),
        compiler_params=_cparams("parallel", "arbitrary"),
        name="matmul_resid",
    )(a, b, x)


def _compress_kernel(x_ref, w1_ref, pos_ref, w2_ref, o_ref):
    ncp = x_ref.shape[2]
    uv = jnp.dot(x_ref[0, 0], w1_ref[0], preferred_element_type=F32)
    bias = jnp.dot(pos_ref[0], w1_ref[0], preferred_element_type=F32)
    top = uv[:, :CMP_HIDDEN] + bias[0:1, :CMP_HIDDEN]
    bot = uv[:, CMP_HIDDEN:]
    bias_bot = bias[1:2, CMP_HIDDEN:]
    bot_next = pltpu.roll(bot, ncp - 1, 0)
    hid = _silu(top + bot_next + bias_bot)
    o_ref[0, 0] = jnp.dot(hid.astype(BF16), w2_ref[0], preferred_element_type=F32).astype(o_ref.dtype)


def compress(xg, w1cat, pos2, w2):
    _, G, ncp, _ = xg.shape
    half = CMP_STRIDE * HEAD_DIM
    return pl.pallas_call(
        _compress_kernel,
        grid=(2, G),
        in_specs=[pl.BlockSpec((1, 1, ncp, half), lambda a, g: (a, g, 0, 0)),
                  pl.BlockSpec((1, half, 2 * CMP_HIDDEN), lambda a, g: (a, 0, 0)),
                  pl.BlockSpec((1, SUBLANES, half), lambda a, g: (a, 0, 0)),
                  pl.BlockSpec((1, CMP_HIDDEN, HEAD_DIM), lambda a, g: (a, 0, 0))],
        out_specs=pl.BlockSpec((1, 1, ncp, HEAD_DIM), lambda a, g: (a, g, 0, 0)),
        out_shape=jax.ShapeDtypeStruct((2, G, ncp, HEAD_DIM), BF16),
        compiler_params=_cparams("parallel", "parallel"),
        name="nsa_compress",
    )(xg, w1cat, pos2, w2)


def _heads_on_lanes(ref, tq):
    return jnp.concatenate([ref[h * HEAD_DIM:(h + 1) * HEAD_DIM, :] for h in range(NSA_HPG)], axis=1)


def _cmp_select_kernel(qT_ref, kc_ref, vcT_ref, ovT_ref, ocT_ref, selT_ref, s_ref, oc_ref, imp_ref,
                       *, tq, rt, n_top):
    qi = pl.program_id(1)
    n_sel = ovT_ref.shape[1]
    q4 = _heads_on_lanes(qT_ref, tq)
    pos = qi * tq + lax.broadcasted_iota(jnp.int32, (1, tq), 1)
    c_max = (qi * tq + tq - CMP_BLOCK) // CMP_STRIDE
    n_rt = (c_max + rt) // rt

    def rows(r):
        return pl.ds(pl.multiple_of(r * rt, rt), rt)

    def scores(r, m):
        s = jnp.dot(kc_ref[0, rows(r), :], q4, preferred_element_type=F32)
        c_idx = r * rt + lax.broadcasted_iota(jnp.int32, (rt, tq), 0)
        bias = jnp.where(c_idx * CMP_STRIDE + (CMP_BLOCK - 1) <= pos, 0.0, NEG_BIG)
        sm = s + jnp.concatenate([bias] * NSA_HPG, axis=1)
        s_ref[rows(r), :] = sm
        return jnp.maximum(m, jnp.max(sm, axis=0, keepdims=True))

    m = lax.fori_loop(0, n_rt, scores, jnp.full((1, NSA_HPG * tq), NEG_BIG, F32))

    def expsum(r, l):
        p = jnp.exp2(s_ref[rows(r), :] - m)
        s_ref[rows(r), :] = p
        return l + jnp.sum(p, axis=0, keepdims=True)

    l = lax.fori_loop(0, n_rt, expsum, jnp.zeros((1, NSA_HPG * tq), F32))
    rl = jnp.where(m > 0.5 * NEG_BIG, 1.0 / l, 0.0)

    oc_ref[...] = jnp.zeros_like(oc_ref)
    imp_ref[...] = jnp.zeros_like(imp_ref)

    def accumulate(r, carry):
        pn = s_ref[rows(r), :] * rl
        oc_ref[...] += jnp.dot(vcT_ref[0, r], pn.astype(BF16), preferred_element_type=F32)
        psum = pn[:, 0:tq]
        for h in range(1, NSA_HPG):
            psum = psum + pn[:, h * tq:(h + 1) * tq]
        p_hi = psum.astype(BF16)
        p_lo = (psum - p_hi.astype(F32)).astype(BF16)
        ovT = ovT_ref[r]
        imp_ref[...] += (jnp.dot(ovT, p_hi, preferred_element_type=F32)
                         + jnp.dot(ovT, p_lo, preferred_element_type=F32))
        return carry

    lax.fori_loop(0, n_rt, accumulate, 0)
    for h in range(NSA_HPG):
        ocT_ref[h * HEAD_DIM:(h + 1) * HEAD_DIM, :] = oc_ref[:, h * tq:(h + 1) * tq]
    imp = imp_ref[...]
    j_idx = lax.broadcasted_iota(jnp.int32, (n_sel, tq), 0)
    cur = pos // SEL_BLOCK
    sel_valid = j_idx * SEL_BLOCK <= pos
    forced = (j_idx == 0) | ((j_idx <= cur) & (j_idx > cur - SEL_LOCAL))
    base = jnp.where(sel_valid & ~forced, imp, -1.0)
    work = base
    jf = j_idx.astype(F32)
    for _ in range(max(n_top - SEL_FORCE_COUNT, 0)):
        mx = jnp.max(work, axis=0, keepdims=True)
        first = jnp.min(jnp.where(work == mx, jf, float(n_sel)), axis=0, keepdims=True)
        work = jnp.where(jf == first, -2.0, work)
    picked = (work != base) & (base >= 0.0)
    selT_ref[0] = jnp.where((forced & sel_valid) | picked, 1.0, 0.0)


def cmp_select(qT, kc, vcT, ovT, S, tq=256):
    G, ncp, _ = kc.shape
    n_rt, n_sel, rt = ovT.shape
    n_top = min(SEL_TOPK, n_sel)
    kern = functools.partial(_cmp_select_kernel, tq=tq, rt=rt, n_top=n_top)
    return pl.pallas_call(
        kern,
        grid=(G, S // tq),
        in_specs=[pl.BlockSpec((NSA_HPG * HEAD_DIM, tq), lambda g, i: (g, i)),
                  pl.BlockSpec((1, ncp, HEAD_DIM), lambda g, i: (g, 0, 0)),
                  pl.BlockSpec((1, n_rt, HEAD_DIM, rt), lambda g, i: (g, 0, 0, 0)),
                  pl.BlockSpec((n_rt, n_sel, rt), lambda g, i: (0, 0, 0))],
        out_specs=[pl.BlockSpec((NSA_HPG * HEAD_DIM, tq), lambda g, i: (g, i)),
                   pl.BlockSpec((1, n_sel, tq), lambda g, i: (g, 0, i))],
        out_shape=[jax.ShapeDtypeStruct((NSA_WIDTH, S), F32),
                   jax.ShapeDtypeStruct((G, n_sel, S), F32)],
        scratch_shapes=[pltpu.VMEM((ncp, NSA_HPG * tq), F32),
                        pltpu.VMEM((HEAD_DIM, NSA_HPG * tq), F32),
                        pltpu.VMEM((n_sel, tq), F32)],
        compiler_params=_cparams("parallel", "arbitrary"),
        name="nsa_cmp_select",
    )(qT, kc, vcT, ovT)


NSA_SUBTILES = 4
NSA_SCRATCH_PER_SUBTILE = 10


class _SelSweep:
    def __init__(self, q4, tpos, lanes, kaug_ref, vselT_ref, selT_ref, scratch, tq, kt):
        self.q4, self.tpos, self.lanes, self.tq, self.kt = q4, tpos, lanes, tq, kt
        self.kaug_ref, self.vselT_ref, self.selT_ref = kaug_ref, vselT_ref, selT_ref
        (self.qaug_ref, s0, s1, x0, x1, p0, p1, self.acc_ref, self.m_ref, self.l_ref) = scratch
        self.s_bufs, self.x_bufs, self.p_bufs = (s0, s1), (x0, x1), (p0, p1)

    def scores(self, i, buf):
        bpt = self.kt // SEL_BLOCK
        sel8 = self.selT_ref[0, pl.ds(pl.multiple_of(i * bpt, bpt), bpt), self.lanes]
        neg = jnp.where(sel8 > 0.5, 0.0, NEG_BIG)
        neg = jnp.concatenate([neg, jnp.zeros((MASK_ROWS - bpt, self.tq), F32)], axis=0).astype(BF16)
        self.qaug_ref[HEAD_DIM:HEAD_DIM + MASK_ROWS, :] = jnp.concatenate([neg] * NSA_HPG, axis=1)
        k_t = self.kaug_ref[0, pl.ds(pl.multiple_of(i * self.kt, self.kt), self.kt), :]
        s = jnp.dot(k_t, self.qaug_ref[...], preferred_element_type=F32)
        self.s_bufs[buf][...] = s
        self.x_bufs[buf][...] = jnp.max(s, axis=0, keepdims=True)

    def values(self, i, p_src):
        return jnp.dot(self.vselT_ref[0, i], p_src[...], preferred_element_type=F32)

    def start(self):
        self.qaug_ref[0:HEAD_DIM, :] = self.q4
        self.qaug_ref[HEAD_DIM:, :] = jnp.zeros((AUG_DEPTH - HEAD_DIM, NSA_HPG * self.tq), BF16)
        self.acc_ref[...] = jnp.zeros_like(self.acc_ref)
        self.m_ref[...] = jnp.full_like(self.m_ref, NEG_BIG)
        self.l_ref[...] = jnp.zeros_like(self.l_ref)
        self.p_bufs[1][...] = jnp.zeros_like(self.p_bufs[1])
        self.scores(0, 0)

    def step(self, t, cur, diagonal):
        nxt = 1 - cur
        if not diagonal:
            self.scores(t + 1, nxt)
        pv_prev = self.values(jnp.maximum(t - 1, 0), self.p_bufs[nxt])
        m_old = self.m_ref[...]
        if diagonal:
            kpos = t * self.kt + lax.broadcasted_iota(jnp.int32, (self.kt, self.tq), 0)
            causal = jnp.where(kpos <= self.tpos, 0.0, NEG_BIG)
            sm = self.s_bufs[cur][...] + jnp.concatenate([causal] * NSA_HPG, axis=1)
            m_new = jnp.maximum(m_old, jnp.max(sm, axis=0, keepdims=True))
        else:
            m_new = jnp.maximum(m_old, self.x_bufs[cur][...])
            sm = self.s_bufs[cur][...]
        alpha = jnp.exp2(m_old - m_new)
        p = jnp.exp2(sm - m_new)
        self.p_bufs[cur][...] = p.astype(BF16)
        self.l_ref[...] = self.l_ref[...] * alpha + jnp.sum(p, axis=0, keepdims=True)
        self.acc_ref[...] = (self.acc_ref[...] + pv_prev) * alpha
        self.m_ref[...] = m_new

    def last_step(self, t, cur):
        self.step(t, cur, True)
        self.acc_ref[...] += self.values(t, self.p_bufs[cur])

    def result(self):
        return self.acc_ref[...] / self.l_ref[...]


def _nsa_main_kernel(qT_ref, kaug_ref, vselT_ref, selT_ref, kwin_ref, vwinT_ref, ocT_ref,
                     gate_ref, z_ref, o_ref, *scratch, tq, kt):
    qi = pl.program_id(1)
    base = qi * (NSA_SUBTILES * tq)
    subs = []
    for a in range(NSA_SUBTILES):
        lanes = slice(a * tq, (a + 1) * tq)
        q4 = jnp.concatenate([qT_ref[h * HEAD_DIM:(h + 1) * HEAD_DIM, lanes] for h in range(NSA_HPG)], axis=1)
        tpos = base + a * tq + lax.broadcasted_iota(jnp.int32, (1, tq), 1)
        sc = scratch[a * NSA_SCRATCH_PER_SUBTILE:(a + 1) * NSA_SCRATCH_PER_SUBTILE]
        subs.append(_SelSweep(q4, tpos, lanes, kaug_ref, vselT_ref, selT_ref, sc, tq, kt))

    last = base // kt
    for sw in subs:
        sw.start()

    def pair(j, carry):
        for cur in range(2):
            for sw in subs:
                sw.step(2 * j + cur, cur, False)
        return carry

    lax.fori_loop(0, last // 2, pair, 0)

    @pl.when(last % 2 == 1)
    def _():
        for sw in subs:
            sw.step(last - 1, 0, False)
        for sw in subs:
            sw.last_step(last, 1)

    @pl.when(last % 2 == 0)
    def _():
        for sw in subs:
            sw.last_step(last, 0)

    for a, sel in enumerate(subs):
        q4, tpos, lanes = sel.q4, sel.tpos, sel.lanes
        o_sel = sel.result()
        s0 = base + a * tq

        n_wt = (WINDOW + tq) // LANES
        t0 = jnp.maximum(s0 // LANES - WINDOW // LANES, 0)
        kw = kwin_ref[pl.ds(pl.multiple_of(t0 * LANES, LANES), WINDOW + tq), :]
        sw = jnp.dot(kw, q4, preferred_element_type=F32)
        wpos = t0 * LANES + lax.broadcasted_iota(jnp.int32, (WINDOW + tq, tq), 0)
        okw = (wpos <= tpos) & (wpos > tpos - WINDOW)
        biasw = jnp.where(okw, 0.0, NEG_BIG)
        sw = sw + jnp.concatenate([biasw] * NSA_HPG, axis=1)
        mw = jnp.max(sw, axis=0, keepdims=True)
        pw = jnp.exp2(sw - mw)
        lw = jnp.sum(pw, axis=0, keepdims=True)
        vwT = jnp.concatenate([vwinT_ref[0, t0 + j] for j in range(n_wt)], axis=1)
        o_win = jnp.dot(vwT, pw.astype(BF16), preferred_element_type=F32) / lw

        rows = slice(a * tq, (a + 1) * tq)
        for h in range(NSA_HPG):
            hs = slice(h * tq, (h + 1) * tq)
            g0 = jax.nn.sigmoid(gate_ref[0, 0, h:h + 1, lanes])
            g1 = jax.nn.sigmoid(gate_ref[1, 0, h:h + 1, lanes])
            g2 = jax.nn.sigmoid(gate_ref[2, 0, h:h + 1, lanes])
            oT = (ocT_ref[h * HEAD_DIM:(h + 1) * HEAD_DIM, lanes] * g0 + o_sel[:, hs] * g1
                  + o_win[:, hs] * g2)
            cs = slice(h * HEAD_DIM, (h + 1) * HEAD_DIM)
            o_ref[rows, cs] = (oT.T * _silu(z_ref[rows, cs])).astype(o_ref.dtype)


MASK_ROWS = 16
AUG_DEPTH = 2 * HEAD_DIM


def nsa_main(qT, kaug, k_all, vselT, selT, vwinT, ocT, gatesT, z, S, tq=128, kt=512):
    G = NSA_GROUPS
    kt = min(kt, S)
    n_sel = S // SEL_BLOCK
    tqs = NSA_SUBTILES * tq
    assert tq == LANES and S % kt == 0 and S >= WINDOW + tq and kt // SEL_BLOCK <= MASK_ROWS
    assert kt % tqs == 0 and S % tqs == 0
    kern = functools.partial(_nsa_main_kernel, tq=tq, kt=kt)
    gw = NSA_HPG * HEAD_DIM
    ql = NSA_HPG * tq
    per_subtile = [pltpu.VMEM((AUG_DEPTH, ql), BF16),
                   pltpu.VMEM((kt, ql), F32), pltpu.VMEM((kt, ql), F32),
                   pltpu.VMEM((1, ql), F32), pltpu.VMEM((1, ql), F32),
                   pltpu.VMEM((kt, ql), BF16), pltpu.VMEM((kt, ql), BF16),
                   pltpu.VMEM((HEAD_DIM, ql), F32),
                   pltpu.VMEM((1, ql), F32), pltpu.VMEM((1, ql), F32)]
    assert len(per_subtile) == NSA_SCRATCH_PER_SUBTILE
    return pl.pallas_call(
        kern,
        grid=(G, S // tqs),
        in_specs=[pl.BlockSpec((gw, tqs), lambda g, i: (g, i)),
                  pl.BlockSpec((1, S, AUG_DEPTH), lambda g, i: (g, 0, 0),
                               pipeline_mode=pl.Buffered(1)),
                  pl.BlockSpec((1, S // kt, HEAD_DIM, kt), lambda g, i: (g, 0, 0, 0),
                               pipeline_mode=pl.Buffered(1)),
                  pl.BlockSpec((1, n_sel, tqs), lambda g, i: (g, 0, i)),
                  pl.BlockSpec((S, HEAD_DIM), lambda g, i: (0, 2 * G + g),
                               pipeline_mode=pl.Buffered(1)),
                  pl.BlockSpec((1, S // LANES, HEAD_DIM, LANES), lambda g, i: (g, 0, 0, 0),
                               pipeline_mode=pl.Buffered(1)),
                  pl.BlockSpec((gw, tqs), lambda g, i: (g, i)),
                  pl.BlockSpec((3, 1, SUBLANES, tqs), lambda g, i: (0, g, 0, i)),
                  pl.BlockSpec((tqs, gw), lambda g, i: (i, g))],
        out_specs=pl.BlockSpec((tqs, gw), lambda g, i: (i, g)),
        out_shape=jax.ShapeDtypeStruct((S, NSA_WIDTH), BF16),
        scratch_shapes=per_subtile * NSA_SUBTILES,
        compiler_params=_cparams("parallel", "arbitrary"),
        name="nsa_main",
    )(qT, kaug, vselT, selT, k_all, vwinT, ocT, gatesT, z)


DN_PAIR = 2 * DN_CHUNK


def _lane_pick(x, lane):
    li = lax.broadcasted_iota(jnp.int32, x.shape, 1)
    col = jnp.sum(jnp.where(li == lane, x, 0.0), axis=1, keepdims=True)
    return jnp.broadcast_to(col, x.shape)


def _conv_silu(cur, halo, w):
    r8 = lax.broadcasted_iota(jnp.int32, (SUBLANES, LANES), 0)
    y = cur * w[CONV_WIDTH - 1:CONV_WIDTH]
    for s in range(1, CONV_WIDTH):
        xs = pltpu.roll(cur, s, 0)
        top = jnp.where(r8 < s, pltpu.roll(halo, s, 0), xs[0:SUBLANES])
        xs = jnp.concatenate([top, xs[SUBLANES:]], axis=0)
        y = y + xs * w[CONV_WIDTH - 1 - s:CONV_WIDTH - s]
    return _silu(y)


def _dn_prep_kernel(q_ref, qh_ref, k_ref, kh_ref, v_ref, vh_ref, wq_ref, wk_ref, wv_ref,
                    small_ref, alog_ref, dtb_ref,
                    u_ref, wqd_ref, ak_ref, gl_ref, *, hb):
    p = pl.program_id(0)
    hblk = pl.program_id(1)
    first = p == 0
    R = DN_PAIR
    ri = lax.broadcasted_iota(jnp.int32, (R, R), 0)
    ci = lax.broadcasted_iota(jnp.int32, (R, R), 1)
    same = (ri // DN_CHUNK) == (ci // DN_CHUNK)
    causal = same & (ci <= ri)
    strict = same & (ci < ri)
    eye = (ri == ci).astype(F32)
    rin = ri % DN_CHUNK
    first_half = ci < DN_CHUNK
    r8 = lax.broadcasted_iota(jnp.int32, (SUBLANES, LANES), 0)

    small = small_ref[...]
    beta_all = jax.nn.sigmoid(small)
    g_all = -jnp.exp(alog_ref[...]) * _softplus(small + dtb_ref[...])

    def halo_of(ref, hs):
        h = ref[:, hs]
        return jnp.where(first, jnp.zeros_like(h), h)

    st = []
    for hh in range(hb):
        hs = slice(hh * LANES, (hh + 1) * LANES)
        head = hblk * hb + hh
        q = _conv_silu(q_ref[:, hs], halo_of(qh_ref, hs), wq_ref[:, hs])
        k = _conv_silu(k_ref[:, hs], halo_of(kh_ref, hs), wk_ref[:, hs])
        v = _conv_silu(v_ref[:, hs], halo_of(vh_ref, hs), wv_ref[:, hs])
        q = q * lax.rsqrt(jnp.sum(q * q, axis=1, keepdims=True) + NORM_EPS) * (HEAD_DIM ** -0.5)
        k = k * lax.rsqrt(jnp.sum(k * k, axis=1, keepdims=True) + NORM_EPS)
        beta = _lane_pick(beta_all, 3 * NSA_HEADS + head)
        gc = _lane_pick(g_all, 3 * NSA_HEADS + DN_HEADS + head)
        sh = 1
        while sh < DN_CHUNK:
            gc = gc + jnp.where(rin >= sh, pltpu.roll(gc, sh, 0), 0.0)
            sh *= 2
        g_last = jnp.concatenate(
            [jnp.broadcast_to(gc[DN_CHUNK - 1:DN_CHUNK, :], (DN_CHUNK, LANES)),
             jnp.broadcast_to(gc[R - 1:R, :], (DN_CHUNK, LANES))], axis=0)
        e_g = jnp.exp(gc)
        decay = jnp.exp(jnp.where(causal, gc - gc.T, -jnp.inf))
        kb = k * beta
        pq = _dot_nt(jnp.concatenate([kb, q], axis=0), k)
        a = jnp.where(strict, pq[:R] * decay, 0.0)
        attn = jnp.where(causal, pq[R:] * decay, 0.0)
        qd = q * e_g
        kdT = (k * jnp.exp(g_last - gc)).T
        wqd_half = (qd[:DN_CHUNK], qd[DN_CHUNK:])
        ak_ref[:, hs] = jnp.concatenate(
            [attn[:DN_CHUNK], jnp.where(first_half, kdT, 0.0),
             attn[DN_CHUNK:], jnp.where(first_half, 0.0, kdT)], axis=0).astype(BF16)
        gl_ref[0, hh] = jnp.where(r8 == 0, jnp.exp(g_last[0:SUBLANES]),
                                  jnp.where(r8 == 1, jnp.exp(g_last[DN_CHUNK:DN_CHUNK + SUBLANES]), 0.0))
        rhs = jnp.concatenate([v * beta, kb * e_g], axis=1).astype(BF16)
        st.append(dict(hs=hs, a=a, rhs=rhs, qd=wqd_half))

    for d in st:
        d["x"] = eye - d["a"]
        d["ap"] = _dot(d["a"], d["a"])
    n_sq = int(math.log2(DN_CHUNK)) - 1
    for s in range(n_sq):
        for d in st:
            if s + 1 < n_sq:
                xa = _dot(jnp.concatenate([d["x"], d["ap"]], axis=0), d["ap"])
                d["x"] = d["x"] + xa[:R]
                d["ap"] = xa[R:]
            else:
                d["x"] = d["x"] + _dot(d["x"], d["ap"])
    for d in st:
        uw = jnp.dot(d["x"].astype(BF16), d["rhs"], preferred_element_type=F32)
        u, w = uw[:, :LANES], uw[:, LANES:]
        qd0, qd1 = d["qd"]
        u_ref[:, d["hs"]] = u
        wqd_ref[:, d["hs"]] = jnp.concatenate(
            [w[:DN_CHUNK], qd0, w[DN_CHUNK:], qd1], axis=0).astype(BF16)


def dn_prep(dnqkv, conv_w, small, alog_row, dtb_row, S, hb=8):
    R = DN_PAIR
    NP = S // R
    H = DN_HEADS
    nhb = H // hb
    bw = hb * LANES
    rpb = R // SUBLANES

    def cur(off):
        return pl.BlockSpec((R, bw), lambda p, h: (p, off * nhb + h))

    def halo(off):
        return pl.BlockSpec((SUBLANES, bw), lambda p, h: (jnp.maximum(p * rpb - 1, 0), off * nhb + h))

    def wspec(off):
        return pl.BlockSpec((CONV_WIDTH, bw), lambda p, h: (0, off * nhb + h))

    row = pl.BlockSpec((1, LANES), lambda p, h: (0, 0))
    kern = functools.partial(_dn_prep_kernel, hb=hb)
    return pl.pallas_call(
        kern,
        grid=(NP, nhb),
        in_specs=[cur(0), halo(0), cur(1), halo(1), cur(2), halo(2), wspec(0), wspec(1), wspec(2),
                  pl.BlockSpec((R, LANES), lambda p, h: (p, 0)), row, row],
        out_specs=[pl.BlockSpec((R, bw), lambda p, h: (p, h)),
                   pl.BlockSpec((2 * R, bw), lambda p, h: (p, h)),
                   pl.BlockSpec((3 * R, bw), lambda p, h: (p, h)),
                   pl.BlockSpec((1, hb, SUBLANES, LANES), lambda p, h: (p, h, 0, 0))],
        out_shape=[jax.ShapeDtypeStruct((S, DN_WIDTH), F32),
                   jax.ShapeDtypeStruct((2 * S, DN_WIDTH), BF16),
                   jax.ShapeDtypeStruct((3 * S, DN_WIDTH), BF16),
                   jax.ShapeDtypeStruct((NP, H, SUBLANES, LANES), F32)],
        compiler_params=_cparams("parallel", "parallel"),
        name="dn_prep",
    )(dnqkv, dnqkv, dnqkv, dnqkv, dnqkv, dnqkv, conv_w, conv_w, conv_w, small, alog_row, dtb_row)


def _dn_scan_kernel(u_ref, wqd_ref, ak_ref, gl_ref, z_ref, ng_ref, o_ref, state_ref):
    p = pl.program_id(0)

    @pl.when(p == 0)
    def _():
        state_ref[...] = jnp.zeros_like(state_ref)

    C = DN_CHUNK
    zeros = jnp.zeros((C, LANES), F32)
    ng = ng_ref[...]
    heads = [slice(h * LANES, (h + 1) * LANES) for h in range(DN_HEADS)]
    st = [state_ref[h] for h in range(DN_HEADS)]
    outs = [[] for _ in range(DN_HEADS)]
    for c in range(2):
        r1 = [jnp.dot(wqd_ref[2 * c * C:2 * (c + 1) * C, hs], st[h].astype(BF16),
                      preferred_element_type=F32) for h, hs in enumerate(heads)]
        r2 = []
        for h, hs in enumerate(heads):
            vn = u_ref[c * C:(c + 1) * C, hs] - r1[h][:C]
            vn_full = jnp.concatenate([vn, zeros] if c == 0 else [zeros, vn], axis=0)
            ak = ak_ref[3 * c * C:3 * (c + 1) * C, hs]
            r2.append(jnp.dot(ak, vn_full.astype(BF16), preferred_element_type=F32))
        for h in range(DN_HEADS):
            outs[h].append(r1[h][C:] + r2[h][:C])
            st[h] = st[h] * gl_ref[0, h, c:c + 1, :] + r2[h][C:]
    for h, hs in enumerate(heads):
        state_ref[h] = st[h]
        o = jnp.concatenate(outs[h], axis=0)
        o = o * lax.rsqrt(jnp.mean(o * o, axis=1, keepdims=True) + NORM_EPS) * ng
        o_ref[:, hs] = (o * _silu(z_ref[:, hs])).astype(o_ref.dtype)


def dn_scan(u, wqd, ak, gl, z_dn, norm_g, S):
    R = DN_PAIR
    NP = S // R
    W = DN_WIDTH
    return pl.pallas_call(
        _dn_scan_kernel,
        grid=(NP,),
        in_specs=[pl.BlockSpec((R, W), lambda p: (p, 0)),
                  pl.BlockSpec((2 * R, W), lambda p: (p, 0)),
                  pl.BlockSpec((3 * R, W), lambda p: (p, 0)),
                  pl.BlockSpec((1, DN_HEADS, SUBLANES, LANES), lambda p: (p, 0, 0, 0)),
                  pl.BlockSpec((R, W), lambda p: (p, 0)),
                  pl.BlockSpec((1, LANES), lambda p: (0, 0))],
        out_specs=pl.BlockSpec((R, W), lambda p: (p, 0)),
        out_shape=jax.ShapeDtypeStruct((S, W), BF16),
        scratch_shapes=[pltpu.VMEM((DN_HEADS, HEAD_DIM, HEAD_DIM), F32)],
        compiler_params=_cparams("arbitrary"),
        name="dn_scan",
    )(u, wqd, ak, gl, z_dn, norm_g.reshape(1, LANES))


def _rope_tables(S):
    inv_freq = ROPE_THETA ** (-jnp.arange(0, ROT_DIM, 2, dtype=F32) / ROT_DIM)
    ang = jnp.arange(S).astype(F32)[:, None] * inv_freq[None, :]
    cos, sin = jnp.cos(ang), jnp.sin(ang)
    half = ROT_DIM // 2
    rest = LANES - ROT_DIM
    c = jnp.concatenate([cos, cos, jnp.ones((S, rest), F32)], axis=1)
    s1 = jnp.concatenate([-sin, jnp.zeros((S, LANES - half), F32)], axis=1)
    s2 = jnp.concatenate([jnp.zeros((S, half), F32), sin, jnp.zeros((S, rest), F32)], axis=1)
    return c, s1, s2


def _overlap_T(S):
    n_cmp = (S - CMP_BLOCK) // CMP_STRIDE + 1
    ncp = S // CMP_STRIDE
    n_sel = S // SEL_BLOCK
    c = np.arange(ncp)[None, :]
    j = np.arange(n_sel)[:, None]
    ov = (c * CMP_STRIDE < j * SEL_BLOCK + SEL_BLOCK) & (c * CMP_STRIDE + CMP_BLOCK - 1 >= j * SEL_BLOCK) & (c < n_cmp)
    rt = min(CMP_ROW_TILE, ncp)
    ov = ov.reshape(n_sel, ncp // rt, rt).transpose(1, 0, 2)
    return jnp.asarray(ov, dtype=BF16)


def _block_onehot(S, kt):
    s = np.arange(S)[:, None]
    r = np.arange(LANES)[None, :]
    return jnp.asarray((s % kt) // SEL_BLOCK == r, dtype=BF16)


def _consts(S):
    return _rope_tables(S), _overlap_T(S), _block_onehot(S, min(SEL_KEY_TILE, S))


_SEG = np.cumsum([0, NSA_WIDTH, 6 * NSA_KV_WIDTH, 3 * NSA_HEADS, NSA_WIDTH, 3 * DN_WIDTH,
                  DN_HEADS, DN_HEADS, DN_WIDTH, 2 * D_MODEL])


def _split_w_in(w):
    seg = lambda i: w[:, _SEG[i]:_SEG[i + 1]]
    kv = seg(1).reshape(D_MODEL, 6, NSA_KV_WIDTH)
    wk = kv[:, 0::2].reshape(D_MODEL, 3 * NSA_KV_WIDTH)
    wv = kv[:, 1::2].reshape(D_MODEL, 3 * NSA_KV_WIDTH)
    n_small = 3 * NSA_HEADS + 2 * DN_HEADS
    small = jnp.concatenate([seg(2), seg(5), seg(6), jnp.zeros((D_MODEL, LANES - n_small), w.dtype)], axis=1)
    parts = dict(q=seg(0), k=wk, v=wv, small=small, z_nsa=seg(3), dn_qkv=seg(4), z_dn=seg(7), merge=seg(8))
    return {n: p.astype(BF16) for n, p in parts.items()}


def _layer(x, norm_g, w_in, cmp_pos, cmp_w1, cmp_w2, dn_conv_w, dn_a_log, dn_dt_bias,
           dn_out_norm_g, w_branch_nsa, w_branch_dn, w_out, consts):
    S = x.shape[0]
    G = NSA_GROUPS
    rope_tabs, ovT, onehot = consts
    wp = _split_w_in(w_in)
    h = rmsnorm(x, norm_g, BF16)

    q = matmul_rope(h, wp["q"], rope_tabs, BF16, out_scale=ATTN_SCALE * math.log2(math.e),
                    name="proj_q")
    k_all = matmul_rope(h, wp["k"], rope_tabs, BF16, name="proj_k")
    v_all = matmul(h, wp["v"], BF16, name="proj_v")
    small = matmul(h, wp["small"], F32, tn=LANES, name="proj_small")
    z_nsa = matmul(h, wp["z_nsa"], F32, name="proj_z_nsa")
    dnqkv = matmul(h, wp["dn_qkv"], F32, name="proj_dn_qkv")
    z_dn = matmul(h, wp["z_dn"], F32, name="proj_z_dn")
    merge = matmul(h, wp["merge"], F32, name="proj_merge")

    ncp = S // CMP_STRIDE
    kvw = NSA_KV_WIDTH

    def chunked(t):
        return t.reshape(ncp, CMP_STRIDE, G, HEAD_DIM).transpose(2, 0, 1, 3).reshape(G, ncp, CMP_STRIDE * HEAD_DIM)

    xg = jnp.stack([chunked(k_all[:, :kvw]), chunked(v_all[:, :kvw])])
    half = CMP_STRIDE * HEAD_DIM
    w1cat = jnp.concatenate([cmp_w1[:, :half], cmp_w1[:, half:]], axis=2).astype(BF16)
    pos_flat = cmp_pos.reshape(2, 2, half)
    pos2 = jnp.concatenate([pos_flat, jnp.zeros((2, SUBLANES - 2, half), F32)], axis=1).astype(BF16)
    kvc = compress(xg, w1cat, pos2, cmp_w2.astype(BF16))
    kc = kvc[0]
    rt = ovT.shape[2]
    vcT = kvc[1].reshape(G, ncp // rt, rt, HEAD_DIM).transpose(0, 1, 3, 2)

    qT = q.T
    ocT, selT = cmp_select(qT, kc, vcT, ovT, S)

    kt = min(SEL_KEY_TILE, S)
    k_sel = k_all[:, kvw:2 * kvw].reshape(S, G, HEAD_DIM).transpose(1, 0, 2)
    kaug = jnp.concatenate([k_sel, jnp.broadcast_to(onehot[None], (G, S, LANES))], axis=2)
    vselT = v_all[:, kvw:2 * kvw].reshape(S // kt, kt, G, HEAD_DIM).transpose(2, 0, 3, 1)
    vwinT = v_all[:, 2 * kvw:].reshape(S // LANES, LANES, G, HEAD_DIM).transpose(2, 0, 3, 1)
    gates = small[:, :3 * NSA_HEADS].reshape(S, G, NSA_HPG, 3).transpose(3, 1, 2, 0)
    gatesT = jnp.concatenate([gates, jnp.zeros((3, G, SUBLANES - NSA_HPG, S), F32)], axis=2)
    a_nsa = nsa_main(qT, kaug, k_all, vselT, selT, vwinT, ocT, gatesT, z_nsa, S, kt=kt)

    n_gate = 3 * NSA_HEADS
    pad_l = n_gate + DN_HEADS
    alog_row = jnp.concatenate([jnp.zeros((pad_l,), F32), dn_a_log.astype(F32),
                                jnp.zeros((LANES - pad_l - DN_HEADS,), F32)]).reshape(1, LANES)
    dtb_row = jnp.concatenate([jnp.zeros((pad_l,), F32), dn_dt_bias.astype(F32),
                               jnp.zeros((LANES - pad_l - DN_HEADS,), F32)]).reshape(1, LANES)
    u, wqd, ak, gl = dn_prep(dnqkv, dn_conv_w, small, alog_row, dtb_row, S)
    a_dn = dn_scan(u, wqd, ak, gl, z_dn, dn_out_norm_g, S)

    mixed = matmul_merge(a_nsa, w_branch_nsa.astype(BF16), a_dn, w_branch_dn.astype(BF16), merge, BF16)
    return matmul_resid(mixed, w_out.astype(BF16), x)


def kernel(x, norm_g, w_in, cmp_pos, cmp_w1, cmp_w2, dn_conv_w, dn_a_log, dn_dt_bias,
           dn_out_norm_g, w_branch_nsa, w_branch_dn, w_out, final_norm_g):
    B, S, D = x.shape
    assert B == 1 and D == D_MODEL
    consts = _consts(S)
    xs = x[0]
    for l in range(norm_g.shape[0]):
        xs = _layer(xs, norm_g[l], w_in[l], cmp_pos[l], cmp_w1[l], cmp_w2[l], dn_conv_w[l],
                    dn_a_log[l], dn_dt_bias[l], dn_out_norm_g[l], w_branch_nsa[l],
                    w_branch_dn[l], w_out[l], consts)
    return rmsnorm(xs, final_norm_g, F32)[None]
```
